```python
import math
import jax
import jax.numpy as jnp
from jax import lax
import numpy as np

D_MODEL = 4096
BATCH = 4
SEQ = 4096
DEPTH = 2
DEC_BATCH = 16
DEC_SEQ = 32
PAST_LEN = 2048

CHUNK = 64
N_BRANCH = 4
D_MIX = D_MODEL // 4
CONV_A_WIDTH = 3
RET_HEADS = 4
RET_DK = D_MIX // RET_HEADS
RET_DV = D_MIX // RET_HEADS
MLSTM_HEADS = 4
MLSTM_DK = D_MIX // MLSTM_HEADS
MLSTM_DV = D_MIX // MLSTM_HEADS
GDN_HEADS = 8
GDN_DK = D_MIX // GDN_HEADS
GDN_DV = D_MIX // GDN_HEADS
GDN_CONV = 4
ROPE_BASE = 10000.0
D_FF = 7 * D_MODEL // 2
N_EXPERTS = 8
TOP_K = 2
D_FF_EXPERT = 7 * D_MODEL // 2
EPS = 1e-6
IN_SIZES = (D_MIX, D_MIX, D_MIX,
            D_MIX, D_MIX, D_MIX, D_MIX,
            D_MIX, D_MIX, D_MIX, D_MIX, MLSTM_HEADS, MLSTM_HEADS,
            3 * D_MIX, D_MIX, GDN_HEADS, GDN_HEADS,
            N_BRANCH * D_MODEL)
SPLIT_POINTS = tuple(int(s) for s in np.cumsum(IN_SIZES)[:-1])
N_IN = sum(IN_SIZES)

kernel_name = 'hybrid_stream_encoder_step'


def _rmsnorm(x, g):
    xf = x.astype(jnp.float32)
    y = xf * lax.rsqrt(jnp.mean(xf * xf, axis=-1, keepdims=True) + EPS)
    return (y * g.astype(jnp.float32)).astype(x.dtype)


def _head_layernorm(h, g):
    B, T, H, d = h.shape
    hc = h - jnp.mean(h, axis=-1, keepdims=True)
    y = hc * lax.rsqrt(jnp.mean(hc * hc, axis=-1, keepdims=True) + EPS)
    return y.reshape(B, T, H * d) * g.astype(jnp.float32)


def _l2norm(x):
    return x * lax.rsqrt(jnp.sum(x * x, axis=-1, keepdims=True) + EPS)


def _heads(t, H):
    return t.astype(jnp.float32).reshape(t.shape[0], t.shape[1], H, -1)


def _rope(x, pos):
    half = x.shape[-1] // 2
    inv = ROPE_BASE ** (-jnp.arange(half, dtype=jnp.float32) / half)
    ang = pos.astype(jnp.float32)[:, None] * inv[None, :]
    cos = jnp.cos(ang)[None, :, None, :]
    sin = jnp.sin(ang)[None, :, None, :]
    x1, x2 = x[..., :half], x[..., half:]
    return jnp.concatenate([x1 * cos - x2 * sin, x1 * sin + x2 * cos], axis=-1)


def _chunk_len(T):
    return CHUNK if T % CHUNK == 0 else T


def _to_chunks(a, L):
    B, T = a.shape[:2]
    a = a.reshape((B, T // L, L) + a.shape[2:])
    return jnp.moveaxis(a, (1, 2), (0, 3))


def _from_chunks(a):
    a = jnp.moveaxis(a, (0, 3), (1, 2))
    return a.reshape((a.shape[0], a.shape[1] * a.shape[2]) + a.shape[3:])


def _causal_conv(u, buf, w):
    K = w.shape[0]
    up = jnp.concatenate([buf.astype(u.dtype), u], axis=1)
    y = lax.conv_general_dilated(up, w[:, None, :].astype(u.dtype), window_strides=(1,), padding='VALID',
                                 dimension_numbers=('NWC', 'WIO', 'NWC'), feature_group_count=u.shape[-1])
    return y, up[:, up.shape[1] - (K - 1):]


def _retention(q, k, v, S0, log_gamma):
    L = _chunk_len(q.shape[1])
    idx = jnp.arange(L)
    diff = (idx[:, None] - idx[None, :]).astype(jnp.float32)
    intra = jnp.exp(jnp.where(idx[:, None] >= idx[None, :], diff[None] * log_gamma[:, None, None], -jnp.inf))
    inter = jnp.exp((idx + 1).astype(jnp.float32)[None, :] * log_gamma[:, None])
    w_k = jnp.exp((L - 1 - idx).astype(jnp.float32)[None, :] * log_gamma[:, None])
    dec = jnp.exp(L * log_gamma)

    def step(S, xs):
        qc, kc, vc = xs
        s = jnp.einsum('bhid,bhjd->bhij', qc, kc) * intra
        o = jnp.einsum('bhij,bhje->bhie', s, vc) + inter[..., None] * jnp.einsum('bhid,bhde->bhie', qc, S)
        S = dec[:, None, None] * S + jnp.einsum('bhjd,bhje->bhde', kc * w_k[..., None], vc)
        return S, o

    S, o = lax.scan(step, S0, (_to_chunks(q, L), _to_chunks(k, L), _to_chunks(v, L)))
    return _from_chunks(o), S


def _mlstm(q, k, v, log_i, log_f, C0, n0, m0):
    L = _chunk_len(q.shape[1])
    idx = jnp.arange(L)
    causal = idx[:, None] >= idx[None, :]

    def step(carry, xs):
        C, n, m = carry
        qc, kc, vc, li, lf = xs
        b = jnp.cumsum(lf, axis=-1)
        logD = jnp.where(causal, b[..., :, None] - b[..., None, :] + li[..., None, :], -jnp.inf)
        inter = b + m[..., None]
        mt = jnp.maximum(inter, jnp.max(logD, axis=-1))
        s = jnp.einsum('bhid,bhjd->bhij', qc, kc) * jnp.exp(logD - mt[..., None])
        g_inter = jnp.exp(inter - mt)
        num = jnp.einsum('bhij,bhje->bhie', s, vc) + g_inter[..., None] * jnp.einsum('bhid,bhde->bhie', qc, C)
        den = jnp.sum(s, axis=-1) + g_inter * jnp.einsum('bhid,bhd->bhi', qc, n)
        h = num / jnp.maximum(jnp.abs(den), 1.0)[..., None]
        m_last = mt[..., -1]
        w_j = jnp.exp(b[..., -1:] - b + li - m_last[..., None])
        dC = jnp.exp(b[..., -1] + m - m_last)
        kw = kc * w_j[..., None]
        C = dC[..., None, None] * C + jnp.einsum('bhjd,bhje->bhde', kw, vc)
        n = dC[..., None] * n + jnp.sum(kw, axis=2)
        return (C, n, m_last), h

    xs = (_to_chunks(q, L), _to_chunks(k, L), _to_chunks(v, L), _to_chunks(log_i, L), _to_chunks(log_f, L))
    (C, n, m), h = lax.scan(step, (C0, n0, m0), xs)
    return _from_chunks(h), C, n, m


def _gated_delta(q, k, v, g, beta, S0):
    L = _chunk_len(q.shape[1])
    idx = jnp.arange(L)
    incl = idx[:, None] >= idx[None, :]
    strict = idx[:, None] > idx[None, :]
    eye = jnp.eye(L, dtype=jnp.float32)

    def step(S, xs):
        qc, kc, vc, gc, bc = xs
        G = jnp.cumsum(gc, axis=-1)
        decay = jnp.exp(jnp.where(incl, G[..., :, None] - G[..., None, :], -jnp.inf))
        A = jnp.where(strict, bc[..., :, None] * jnp.einsum('bhid,bhjd->bhij', kc, kc) * decay, 0.0)
        rhs = jnp.concatenate([vc * bc[..., None], kc * (bc * jnp.exp(G))[..., None]], axis=-1)
        sol = lax.linalg.triangular_solve(eye + A, rhs, left_side=True, lower=True)
        u, w = sol[..., :GDN_DV], sol[..., GDN_DV:]
        delta = u - jnp.einsum('bhik,bhkv->bhiv', w, S)
        o = (jnp.einsum('bhij,bhjv->bhiv', jnp.einsum('bhik,bhjk->bhij', qc, kc) * decay, delta)
             + jnp.exp(G)[..., None] * jnp.einsum('bhik,bhkv->bhiv', qc, S))
        G_last = G[..., -1]
        S = (jnp.exp(G_last)[..., None, None] * S
             + jnp.einsum('bhjk,bhjv->bhkv', kc * jnp.exp(G_last[..., None] - G)[..., None], delta))
        return S, o

    S, o = lax.scan(step, S0, tuple(_to_chunks(a, L) for a in (q, k, v, g, beta)))
    return _from_chunks(o), S


def _mixer(h, pos, layer_state, w_in, conv_a_w, ret_norm_g, mlstm_b_i, mlstm_b_f, mlstm_norm_g,
           gdn_conv_w, gdn_a_log, gdn_dt_bias, gdn_norm_g, w_branch, w_out):
    f32 = jnp.float32
    B, T, _ = h.shape
    dt = h.dtype
    conv_a_buf, ret_S, m_C, m_n, m_m, g_buf, g_S = layer_state
    proj = h @ w_in
    (a_b, a_c, a_h, r_q, r_k, r_v, r_g, m_q, m_k, m_v, m_o, m_i, m_f,
     g_qkv, g_z, g_beta, g_a, gate_pre) = jnp.split(proj, SPLIT_POINTS, axis=-1)

    a_conv, conv_a_new = _causal_conv(a_c * a_h, conv_a_buf, conv_a_w)
    y_a = a_b * a_conv

    rq = _rope(_heads(r_q, RET_HEADS), pos)
    rk = _rope(_heads(r_k, RET_HEADS), pos) * RET_DK ** -0.5
    log_gamma = jnp.log1p(-jnp.exp2(-5.0 - jnp.arange(RET_HEADS, dtype=f32)))
    ro, ret_S_new = _retention(rq, rk, _heads(r_v, RET_HEADS), ret_S.astype(f32), log_gamma)
    y_r = (jax.nn.silu(r_g.astype(f32)) * _head_layernorm(ro, ret_norm_g)).astype(dt)

    mk = _heads(m_k, MLSTM_HEADS) * MLSTM_DK ** -0.5
    log_i = m_i.astype(f32) + mlstm_b_i.astype(f32)
    log_f = jax.nn.log_sigmoid(m_f.astype(f32) + mlstm_b_f.astype(f32))
    mh, C_new, n_new, m_new = _mlstm(_heads(m_q, MLSTM_HEADS), mk, _heads(m_v, MLSTM_HEADS), log_i, log_f,
                                     m_C.astype(f32), m_n.astype(f32), m_m.astype(f32))
    y_m = (jax.nn.sigmoid(m_o.astype(f32)) * _head_layernorm(mh, mlstm_norm_g)).astype(dt)

    gqkv, g_buf_new = _causal_conv(g_qkv, g_buf, gdn_conv_w)
    gqkv = jax.nn.silu(gqkv.astype(f32))
    gq, gk, gv = jnp.split(gqkv, 3, axis=-1)
    gq = _l2norm(_heads(gq, GDN_HEADS)) * GDN_DK ** -0.5
    gk = _l2norm(_heads(gk, GDN_HEADS))
    beta = jax.nn.sigmoid(g_beta.astype(f32))
    g_log = -jnp.exp(gdn_a_log.astype(f32)) * jax.nn.softplus(g_a.astype(f32) + gdn_dt_bias.astype(f32))
    go, g_S_new = _gated_delta(gq, gk, _heads(gv, GDN_HEADS), g_log, beta, g_S.astype(f32))
    go = go * lax.rsqrt(jnp.mean(go * go, axis=-1, keepdims=True) + EPS) * gdn_norm_g.astype(f32)
    y_g = (go.reshape(B, T, D_MIX) * jax.nn.silu(g_z.astype(f32))).astype(dt)

    gate = jax.nn.sigmoid(gate_pre.astype(f32)).astype(dt).reshape(B, T, N_BRANCH, D_MODEL)
    merged = gate[:, :, 0] * (y_a @ w_branch[0])
    for i, y in enumerate((y_r, y_m, y_g), start=1):
        merged = merged + gate[:, :, i] * (y @ w_branch[i])
    out = merged @ w_out
    new_state = (conv_a_new, ret_S_new.astype(dt), C_new.astype(dt), n_new.astype(dt), m_new.astype(dt),
                 g_buf_new, g_S_new.astype(dt))
    return out, new_state


def _swiglu(x, w_gu, w_down):
    g, u = jnp.split(x @ w_gu, 2, axis=-1)
    return (jax.nn.silu(g) * u) @ w_down


def _moe(x, w_router, w_gu, w_down):
    logits = (x @ w_router).astype(jnp.float32)
    top_v, top_i = lax.top_k(logits, TOP_K)
    weights = jax.nn.softmax(top_v, axis=-1)
    comb = jnp.einsum('btk,btke->bte', weights, jax.nn.one_hot(top_i, N_EXPERTS, dtype=jnp.float32)).astype(x.dtype)
    out = comb[..., 0:1] * _swiglu(x, w_gu[0], w_down[0])
    for e in range(1, N_EXPERTS):
        out = out + comb[..., e:e + 1] * _swiglu(x, w_gu[e], w_down[e])
    return out


def _trunk(x, pos, states, norm1_g, norm2_g, norm_f_g, w_in, conv_a_w, ret_norm_g, mlstm_b_i, mlstm_b_f,
           mlstm_norm_g, gdn_conv_w, gdn_a_log, gdn_dt_bias, gdn_norm_g, w_branch, w_out,
           ffn_w_gu, ffn_w_down, moe_w_router, moe_w_gu, moe_w_down):
    collected = [[] for _ in states]
    for l in range(DEPTH):
        h = _rmsnorm(x, norm1_g[l])
        mix, new_st = _mixer(h, pos, tuple(s[l] for s in states), w_in[l], conv_a_w[l], ret_norm_g[l],
                             mlstm_b_i[l], mlstm_b_f[l], mlstm_norm_g[l], gdn_conv_w[l], gdn_a_log[l],
                             gdn_dt_bias[l], gdn_norm_g[l], w_branch[l], w_out[l])
        x = x + mix
        h = _rmsnorm(x, norm2_g[l])
        if l % 2 == 0:
            x = x + _swiglu(h, ffn_w_gu[l // 2], ffn_w_down[l // 2])
        else:
            x = x + _moe(h, moe_w_router[l // 2], moe_w_gu[l // 2], moe_w_down[l // 2])
        for i, s in enumerate(new_st):
            collected[i].append(s)
    return _rmsnorm(x, norm_f_g), tuple(jnp.stack(c) for c in collected)


def setup_inputs(seed: int = 0) -> dict:
    key = jax.random.key(seed)
    ks = jax.random.split(key, 32)
    f32 = jnp.float32

    def nrm(k, shape, scale):
        return jax.random.normal(k, shape, f32) * scale

    n_dense = (DEPTH + 1) // 2
    n_moe = DEPTH // 2
    dt_init = jnp.exp(jax.random.uniform(ks[22], (DEPTH, GDN_HEADS), f32, minval=math.log(1e-3), maxval=math.log(1e-1)))
    return {
        'x_prompt': nrm(ks[0], (BATCH, SEQ, D_MODEL), 1.0),
        'x_sample': nrm(ks[1], (DEC_BATCH, DEC_SEQ, D_MODEL), 1.0),
        'state_conv_a': nrm(ks[2], (DEPTH, DEC_BATCH, CONV_A_WIDTH - 1, D_MIX), 1.0),
        'state_ret': nrm(ks[3], (DEPTH, DEC_BATCH, RET_HEADS, RET_DK, RET_DV), 0.3),
        'state_mlstm_c': nrm(ks[4], (DEPTH, DEC_BATCH, MLSTM_HEADS, MLSTM_DK, MLSTM_DV), 0.3),
        'state_mlstm_n': nrm(ks[5], (DEPTH, DEC_BATCH, MLSTM_HEADS, MLSTM_DK), 0.3),
        'state_mlstm_m': nrm(ks[6], (DEPTH, DEC_BATCH, MLSTM_HEADS), 1.0),
        'state_gdn_conv': nrm(ks[7], (DEPTH, DEC_BATCH, GDN_CONV - 1, 3 * D_MIX), 1.0),
        'state_gdn': nrm(ks[8], (DEPTH, DEC_BATCH, GDN_HEADS, GDN_DK, GDN_DV), 0.3),
        'norm1_g': 1.0 + nrm(ks[9], (DEPTH, D_MODEL), 0.01),
        'norm2_g': 1.0 + nrm(ks[10], (DEPTH, D_MODEL), 0.01),
        'norm_f_g': 1.0 + nrm(ks[11], (D_MODEL,), 0.01),
        'w_in': nrm(ks[12], (DEPTH, D_MODEL, N_IN), D_MODEL ** -0.5),
        'conv_a_w': nrm(ks[13], (DEPTH, CONV_A_WIDTH, D_MIX), CONV_A_WIDTH ** -0.5),
        'ret_norm_g': 1.0 + nrm(ks[14], (DEPTH, D_MIX), 0.01),
        'mlstm_b_i': nrm(ks[15], (DEPTH, MLSTM_HEADS), 0.1),
        'mlstm_b_f': jnp.linspace(3.0, 6.0, MLSTM_HEADS, dtype=f32)[None, :] + nrm(ks[16], (DEPTH, MLSTM_HEADS), 0.01),
        'mlstm_norm_g': 1.0 + nrm(ks[17], (DEPTH, D_MIX), 0.01),
        'gdn_conv_w': nrm(ks[18], (DEPTH, GDN_CONV, 3 * D_MIX), GDN_CONV ** -0.5),
        'gdn_a_log': jnp.log(jax.random.uniform(ks[19], (DEPTH, GDN_HEADS), f32, minval=1.0, maxval=16.0)),
        'gdn_dt_bias': dt_init + jnp.log(-jnp.expm1(-dt_init)),
        'gdn_norm_g': 1.0 + nrm(ks[20], (DEPTH, GDN_DV), 0.01),
        'w_branch': nrm(ks[21], (DEPTH, N_BRANCH, D_MIX, D_MODEL), D_MIX ** -0.5),
        'w_out': nrm(ks[23], (DEPTH, D_MODEL, D_MODEL), D_MODEL ** -0.5),
        'ffn_w_gu': nrm(ks[24], (n_dense, D_MODEL, 2 * D_FF), D_MODEL ** -0.5),
        'ffn_w_down': nrm(ks[25], (n_dense, D_FF, D_MODEL), D_FF ** -0.5),
        'moe_w_router': nrm(ks[26], (n_moe, D_MODEL, N_EXPERTS), D_MODEL ** -0.5),
        'moe_w_gu': nrm(ks[27], (n_moe, N_EXPERTS, D_MODEL, 2 * D_FF_EXPERT), D_MODEL ** -0.5),
        'moe_w_down': nrm(ks[28], (n_moe, N_EXPERTS, D_FF_EXPERT, D_MODEL), D_FF_EXPERT ** -0.5),
    }


def reference(x_prompt, x_sample, state_conv_a, state_ret, state_mlstm_c, state_mlstm_n, state_mlstm_m,
              state_gdn_conv, state_gdn, norm1_g, norm2_g, norm_f_g, w_in, conv_a_w, ret_norm_g,
              mlstm_b_i, mlstm_b_f, mlstm_norm_g, gdn_conv_w, gdn_a_log, gdn_dt_bias, gdn_norm_g,
              w_branch, w_out, ffn_w_gu, ffn_w_down, moe_w_router, moe_w_gu, moe_w_down):
    weights = (norm1_g, norm2_g, norm_f_g, w_in, conv_a_w, ret_norm_g, mlstm_b_i, mlstm_b_f, mlstm_norm_g,
               gdn_conv_w, gdn_a_log, gdn_dt_bias, gdn_norm_g, w_branch, w_out, ffn_w_gu, ffn_w_down,
               moe_w_router, moe_w_gu, moe_w_down)
    dt = x_prompt.dtype
    bp = x_prompt.shape[0]
    init = (jnp.zeros((DEPTH, bp, CONV_A_WIDTH - 1, D_MIX), dt),
            jnp.zeros((DEPTH, bp, RET_HEADS, RET_DK, RET_DV), dt),
            jnp.zeros((DEPTH, bp, MLSTM_HEADS, MLSTM_DK, MLSTM_DV), dt),
            jnp.zeros((DEPTH, bp, MLSTM_HEADS, MLSTM_DK), dt),
            jnp.zeros((DEPTH, bp, MLSTM_HEADS), dt),
            jnp.zeros((DEPTH, bp, GDN_CONV - 1, 3 * D_MIX), dt),
            jnp.zeros((DEPTH, bp, GDN_HEADS, GDN_DK, GDN_DV), dt))
    y_prompt, p_st = _trunk(x_prompt, jnp.arange(x_prompt.shape[1]), init, *weights)
    sample_state = (state_conv_a, state_ret, state_mlstm_c, state_mlstm_n, state_mlstm_m, state_gdn_conv, state_gdn)
    y_sample, s_st = _trunk(x_sample, PAST_LEN + jnp.arange(x_sample.shape[1]), sample_state, *weights)
    p_conv_a, p_ret, p_mc, p_mn, p_mm, p_gconv, p_gdn = p_st
    s_conv_a, s_ret, s_mc, s_mn, s_mm, s_gconv, s_gdn = s_st
    return (y_prompt, y_sample, p_conv_a, s_conv_a, p_ret, s_ret, p_mc, s_mc, p_mn, s_mn,
            p_mm, s_mm, p_gconv, s_gconv, p_gdn, s_gdn)
```

```python
import functools
import math

import jax
import jax.numpy as jnp
import numpy as np
from jax import lax
from jax.experimental import pallas as pl
from jax.experimental.pallas import tpu as pltpu

D_MODEL = 4096
DEPTH = 2
CHUNK = 64
N_BRANCH = 4
D_MIX = D_MODEL // 4
CONV_A_WIDTH = 3
RET_HEADS = 4
RET_DK = D_MIX // RET_HEADS
MLSTM_HEADS = 4
MLSTM_DK = D_MIX // MLSTM_HEADS
GDN_HEADS = 8
GDN_DK = D_MIX // GDN_HEADS
GDN_DV = D_MIX // GDN_HEADS
GDN_CONV = 4
ROPE_BASE = 10000.0
D_FF = 7 * D_MODEL // 2
N_EXPERTS = 8
TOP_K = 2
EPS = 1e-6
PAST_LEN = 2048

_OFF_MI = 11 * D_MIX
_OFF_GQKV = _OFF_MI + 2 * MLSTM_HEADS
_OFF_GBETA = _OFF_GQKV + 4 * D_MIX
_OFF_GATE = _OFF_GBETA + 2 * GDN_HEADS
N_IN = _OFF_GATE + N_BRANCH * D_MODEL
N_MAIN = 15 * D_MIX + N_BRANCH * D_MODEL
N_SMALL = 128

V7X_VMEM_LIMIT = 56 * 1024 * 1024
BM = 1024
BM_MOE = 512


def _cparams(sem):
    return pltpu.CompilerParams(dimension_semantics=sem, vmem_limit_bytes=V7X_VMEM_LIMIT)


def _rmsnorm_kernel(x_ref, g_ref, o_ref):
    x = x_ref[...]
    y = x * lax.rsqrt(jnp.mean(x * x, axis=-1, keepdims=True) + EPS)
    o_ref[...] = (y * g_ref[...]).astype(o_ref.dtype)


def _rmsnorm(x, g, out_dtype, bm=256):
    m, d = x.shape
    return pl.pallas_call(
        _rmsnorm_kernel,
        grid=(m // bm,),
        in_specs=[pl.BlockSpec((bm, d), lambda i: (i, 0)), pl.BlockSpec((1, d), lambda i: (0, 0))],
        out_specs=pl.BlockSpec((bm, d), lambda i: (i, 0)),
        out_shape=jax.ShapeDtypeStruct((m, d), out_dtype),
        compiler_params=_cparams(("parallel",)),
        name="rmsnorm",
    )(x, g.reshape(1, d))


def _mm_kernel(x_ref, w_ref, o_ref):
    o_ref[...] = jnp.dot(x_ref[...], w_ref[...], preferred_element_type=jnp.float32).astype(o_ref.dtype)


def _mm(x, w, out_dtype, bm, bn, name):
    m, k = x.shape
    n = w.shape[1]
    return pl.pallas_call(
        _mm_kernel,
        grid=(m // bm, n // bn),
        in_specs=[pl.BlockSpec((bm, k), lambda i, j: (i, 0)), pl.BlockSpec((k, bn), lambda i, j: (0, j))],
        out_specs=pl.BlockSpec((bm, bn), lambda i, j: (i, j)),
        out_shape=jax.ShapeDtypeStruct((m, n), out_dtype),
        compiler_params=_cparams(("parallel", "arbitrary")),
        name=name,
    )(x, w)


def _mm_res_kernel(x_ref, w_ref, r_ref, o_ref):
    o_ref[...] = r_ref[...] + jnp.dot(x_ref[...], w_ref[...], preferred_element_type=jnp.float32)


def _mm_res(x, w, res, bm, bn, name):
    m, k = x.shape
    n = w.shape[1]
    return pl.pallas_call(
        _mm_res_kernel,
        grid=(m // bm, n // bn),
        in_specs=[pl.BlockSpec((bm, k), lambda i, j: (i, 0)), pl.BlockSpec((k, bn), lambda i, j: (0, j)),
                  pl.BlockSpec((bm, bn), lambda i, j: (i, j))],
        out_specs=pl.BlockSpec((bm, bn), lambda i, j: (i, j)),
        out_shape=jax.ShapeDtypeStruct((m, n), jnp.float32),
        compiler_params=_cparams(("parallel", "arbitrary")),
        name=name,
    )(x, w, res)


def _merge_kernel(y_ref, g0_ref, g1_ref, g2_ref, g3_ref, w_ref, o_ref):
    acc = None
    for b, g_ref in enumerate((g0_ref, g1_ref, g2_ref, g3_ref)):
        t = jax.nn.sigmoid(g_ref[...]) * jnp.dot(y_ref[b], w_ref[b], preferred_element_type=jnp.float32)
        acc = t if acc is None else acc + t
    o_ref[...] = acc.astype(o_ref.dtype)


def _merge(y4, proj_main, w_branch, bm, bn):
    _, m, dm = y4.shape
    n = w_branch.shape[2]
    gate_blk0 = (15 * D_MIX) // bn
    per_branch = n // bn

    def gate_spec(b):
        return pl.BlockSpec((bm, bn), lambda i, j: (i, gate_blk0 + b * per_branch + j))

    return pl.pallas_call(
        _merge_kernel,
        grid=(m // bm, n // bn),
        in_specs=[pl.BlockSpec((N_BRANCH, bm, dm), lambda i, j: (0, i, 0)),
                  gate_spec(0), gate_spec(1), gate_spec(2), gate_spec(3),
                  pl.BlockSpec((N_BRANCH, dm, bn), lambda i, j: (0, 0, j))],
        out_specs=pl.BlockSpec((bm, bn), lambda i, j: (i, j)),
        out_shape=jax.ShapeDtypeStruct((m, n), jnp.bfloat16),
        compiler_params=_cparams(("parallel", "arbitrary")),
        name="branch_merge",
    )(y4, proj_main, proj_main, proj_main, proj_main, w_branch)


def _gu_kernel(x_ref, wg_ref, wu_ref, o_ref):
    x = x_ref[...]
    g = jnp.dot(x, wg_ref[...], preferred_element_type=jnp.float32)
    u = jnp.dot(x, wu_ref[...], preferred_element_type=jnp.float32)
    o_ref[...] = (g * jax.nn.sigmoid(g) * u).astype(o_ref.dtype)


def _ffn_gu(x, w_gu, bm, bn):
    m, k = x.shape
    f = w_gu.shape[1] // 2
    nb = f // bn
    return pl.pallas_call(
        _gu_kernel,
        grid=(m // bm, nb),
        in_specs=[pl.BlockSpec((bm, k), lambda i, j: (i, 0)),
                  pl.BlockSpec((k, bn), lambda i, j: (0, j)),
                  pl.BlockSpec((k, bn), lambda i, j: (0, nb + j))],
        out_specs=pl.BlockSpec((bm, bn), lambda i, j: (i, j)),
        out_shape=jax.ShapeDtypeStruct((m, f), jnp.bfloat16),
        compiler_params=_cparams(("parallel", "arbitrary")),
        name="ffn_gate_up",
    )(x, w_gu, w_gu)


def _down_kernel(x_ref, w_ref, r_ref, o_ref, acc_ref):
    k = pl.program_id(2)

    @pl.when(k == 0)
    def _():
        acc_ref[...] = r_ref[...]

    acc_ref[...] += jnp.dot(x_ref[...], w_ref[...], preferred_element_type=jnp.float32)

    @pl.when(k == pl.num_programs(2) - 1)
    def _():
        o_ref[...] = acc_ref[...]


def _ffn_down(act, w_down, res, bm, bn, bk):
    m, f = act.shape
    n = w_down.shape[1]
    return pl.pallas_call(
        _down_kernel,
        grid=(m // bm, n // bn, f // bk),
        in_specs=[pl.BlockSpec((bm, bk), lambda i, j, k: (i, k)),
                  pl.BlockSpec((bk, bn), lambda i, j, k: (k, j)),
                  pl.BlockSpec((bm, bn), lambda i, j, k: (i, j))],
        out_specs=pl.BlockSpec((bm, bn), lambda i, j, k: (i, j)),
        out_shape=jax.ShapeDtypeStruct((m, n), jnp.float32),
        scratch_shapes=[pltpu.VMEM((bm, bn), jnp.float32)],
        compiler_params=_cparams(("parallel", "arbitrary", "arbitrary")),
        name="ffn_down",
    )(act, w_down, res)


def _router_kernel(x_ref, g_ref, w_ref, h_ref, l_ref):
    x = x_ref[...]
    h = x * lax.rsqrt(jnp.mean(x * x, axis=-1, keepdims=True) + EPS) * g_ref[...]
    h_ref[...] = h.astype(h_ref.dtype)
    l_ref[...] = jnp.dot(h, w_ref[...], preferred_element_type=jnp.float32, precision=lax.Precision.HIGHEST)


def _norm_router(x, g, w_router_pad, bm=256):
    m, d = x.shape
    ne = w_router_pad.shape[1]
    return pl.pallas_call(
        _router_kernel,
        grid=(m // bm,),
        in_specs=[pl.BlockSpec((bm, d), lambda i: (i, 0)), pl.BlockSpec((1, d), lambda i: (0, 0)),
                  pl.BlockSpec((d, ne), lambda i: (0, 0))],
        out_specs=[pl.BlockSpec((bm, d), lambda i: (i, 0)), pl.BlockSpec((bm, ne), lambda i: (i, 0))],
        out_shape=[jax.ShapeDtypeStruct((m, d), jnp.bfloat16), jax.ShapeDtypeStruct((m, ne), jnp.float32)],
        compiler_params=_cparams(("parallel",)),
        name="norm_router",
    )(x, g.reshape(1, d), w_router_pad)


def _moe_gu_kernel(te_ref, nt_ref, x_ref, wg_ref, wu_ref, o_ref):
    @pl.when(pl.program_id(0) < nt_ref[0])
    def _():
        x = x_ref[...]
        g = jnp.dot(x, wg_ref[...], preferred_element_type=jnp.float32)
        u = jnp.dot(x, wu_ref[...], preferred_element_type=jnp.float32)
        o_ref[...] = (g * jax.nn.sigmoid(g) * u).astype(o_ref.dtype)

    @pl.when(pl.program_id(0) >= nt_ref[0])
    def _():
        o_ref[...] = jnp.zeros_like(o_ref)


def _moe_gu(tile_expert, n_tiles, xs, w_gu, bm, bn):
    m, k = xs.shape
    f = w_gu.shape[2] // 2
    nb = f // bn
    grid_spec = pltpu.PrefetchScalarGridSpec(
        num_scalar_prefetch=2,
        grid=(m // bm, nb),
        in_specs=[pl.BlockSpec((bm, k), lambda t, j, te, nt: (t, 0)),
                  pl.BlockSpec((None, k, bn), lambda t, j, te, nt: (te[t], 0, j)),
                  pl.BlockSpec((None, k, bn), lambda t, j, te, nt: (te[t], 0, nb + j))],
        out_specs=pl.BlockSpec((bm, bn), lambda t, j, te, nt: (t, j)),
    )
    return pl.pallas_call(
        _moe_gu_kernel,
        grid_spec=grid_spec,
        out_shape=jax.ShapeDtypeStruct((m, f), jnp.bfloat16),
        compiler_params=_cparams(("parallel", "arbitrary")),
        name="moe_gate_up",
    )(tile_expert, n_tiles, xs, w_gu, w_gu)


def _moe_down_kernel(te_ref, nt_ref, x_ref, w_ref, o_ref, acc_ref):
    k = pl.program_id(2)

    @pl.when(k == 0)
    def _():
        acc_ref[...] = jnp.zeros_like(acc_ref)

    @pl.when(pl.program_id(0) < nt_ref[0])
    def _():
        acc_ref[...] += jnp.dot(x_ref[...], w_ref[...], preferred_element_type=jnp.float32)

    @pl.when(k == pl.num_programs(2) - 1)
    def _():
        o_ref[...] = acc_ref[...]


def _moe_down(tile_expert, n_tiles, act, w_down, bm, bn, bk):
    m, f = act.shape
    n = w_down.shape[2]
    grid_spec = pltpu.PrefetchScalarGridSpec(
        num_scalar_prefetch=2,
        grid=(m // bm, n // bn, f // bk),
        in_specs=[pl.BlockSpec((bm, bk), lambda t, j, k, te, nt: (t, k)),
                  pl.BlockSpec((None, bk, bn), lambda t, j, k, te, nt: (te[t], k, j))],
        out_specs=pl.BlockSpec((bm, bn), lambda t, j, k, te, nt: (t, j)),
        scratch_shapes=[pltpu.VMEM((bm, bn), jnp.float32)],
    )
    return pl.pallas_call(
        _moe_down_kernel,
        grid_spec=grid_spec,
        out_shape=jax.ShapeDtypeStruct((m, n), jnp.float32),
        compiler_params=_cparams(("parallel", "arbitrary", "arbitrary")),
        name="moe_down",
    )(tile_expert, n_tiles, act, w_down)


def _moe(x, m_valid, norm_g, w_router, w_gu, w_down):
    m, d = x.shape
    bm = BM_MOE
    wr = jnp.zeros((d, N_SMALL), jnp.float32).at[:, :N_EXPERTS].set(w_router)
    h, logits = _norm_router(x, norm_g, wr)
    logits = logits[:m_valid, :N_EXPERTS]
    top_v, top_i = lax.top_k(logits, TOP_K)
    weights = jax.nn.softmax(top_v, axis=-1)
    n_pairs = m_valid * TOP_K
    e_flat = top_i.reshape(n_pairs)
    onehot = (e_flat[:, None] == jnp.arange(N_EXPERTS)[None, :]).astype(jnp.int32)
    csum = jnp.cumsum(onehot, axis=0)
    counts = csum[-1]
    rank = jnp.sum(csum * onehot, axis=1) - 1
    padded = ((counts + bm - 1) // bm) * bm
    start = jnp.cumsum(padded) - padded
    dest = start[e_flat] + rank
    n_slots = (n_pairs // bm + N_EXPERTS) * bm
    n_tiles_max = n_slots // bm
    n_tiles = (jnp.sum(padded) // bm).astype(jnp.int32)
    src_tok = jnp.zeros((n_slots,), jnp.int32).at[dest].set(jnp.arange(n_pairs, dtype=jnp.int32) // TOP_K)
    tile_start = jnp.arange(n_tiles_max, dtype=jnp.int32) * bm
    ends = jnp.cumsum(padded)
    tile_expert = jnp.sum((tile_start[:, None] >= ends[None, :]).astype(jnp.int32), axis=1)
    last_e = jnp.max(jnp.where(counts > 0, jnp.arange(N_EXPERTS), 0)).astype(jnp.int32)
    tile_expert = jnp.minimum(tile_expert, last_e).astype(jnp.int32)
    xs = jnp.take(h, src_tok, axis=0)
    act = _moe_gu(tile_expert, n_tiles.reshape(1), xs, w_gu, bm, 512)
    ys = _moe_down(tile_expert, n_tiles.reshape(1), act, w_down, bm, 1024, 2048)
    yp = jnp.take(ys, dest, axis=0).reshape(m_valid, TOP_K, d)
    out = x[:m_valid] + weights[:, 0:1] * yp[:, 0] + weights[:, 1:2] * yp[:, 1]
    if m_valid < m:
        out = jnp.concatenate([out, x[m_valid:]], axis=0)
    return out


def _head_layernorm(h, g):
    B, T, H, d = h.shape
    hc = h - jnp.mean(h, axis=-1, keepdims=True)
    y = hc * lax.rsqrt(jnp.mean(hc * hc, axis=-1, keepdims=True) + EPS)
    return y.reshape(B, T, H * d) * g


def _l2norm(x):
    return x * lax.rsqrt(jnp.sum(x * x, axis=-1, keepdims=True) + EPS)


def _heads(t, H):
    return t.reshape(t.shape[0], t.shape[1], H, -1)


def _rope(x, pos):
    half = x.shape[-1] // 2
    inv = ROPE_BASE ** (-jnp.arange(half, dtype=jnp.float32) / half)
    ang = pos.astype(jnp.float32)[:, None] * inv[None, :]
    cos = jnp.cos(ang)[None, :, None, :]
    sin = jnp.sin(ang)[None, :, None, :]
    x1, x2 = x[..., :half], x[..., half:]
    return jnp.concatenate([x1 * cos - x2 * sin, x1 * sin + x2 * cos], axis=-1)


def _chunk_len(T):
    return CHUNK if T % CHUNK == 0 else T


def _to_chunks(a, L):
    B, T = a.shape[:2]
    a = a.reshape((B, T // L, L) + a.shape[2:])
    return jnp.moveaxis(a, (1, 2), (0, 3))


def _from_chunks(a):
    a = jnp.moveaxis(a, (0, 3), (1, 2))
    return a.reshape((a.shape[0], a.shape[1] * a.shape[2]) + a.shape[3:])


def _causal_conv(u, buf, w):
    K = w.shape[0]
    up = jnp.concatenate([buf, u], axis=1)
    T = u.shape[1]
    y = sum(up[:, i:i + T] * w[i][None, None, :] for i in range(K))
    return y, up[:, up.shape[1] - (K - 1):]


def _retention(q, k, v, S0, log_gamma):
    L = _chunk_len(q.shape[1])
    idx = jnp.arange(L)
    diff = (idx[:, None] - idx[None, :]).astype(jnp.float32)
    intra = jnp.exp(jnp.where(idx[:, None] >= idx[None, :], diff[None] * log_gamma[:, None, None], -jnp.inf))
    inter = jnp.exp((idx + 1).astype(jnp.float32)[None, :] * log_gamma[:, None])
    w_k = jnp.exp((L - 1 - idx).astype(jnp.float32)[None, :] * log_gamma[:, None])
    dec = jnp.exp(L * log_gamma)

    def step(S, xs):
        qc, kc, vc = xs
        s = jnp.einsum('bhid,bhjd->bhij', qc, kc) * intra
        o = jnp.einsum('bhij,bhje->bhie', s, vc) + inter[..., None] * jnp.einsum('bhid,bhde->bhie', qc, S)
        S = dec[:, None, None] * S + jnp.einsum('bhjd,bhje->bhde', kc * w_k[..., None], vc)
        return S, o

    S, o = lax.scan(step, S0, (_to_chunks(q, L), _to_chunks(k, L), _to_chunks(v, L)))
    return _from_chunks(o), S


def _mlstm(q, k, v, log_i, log_f, C0, n0, m0):
    L = _chunk_len(q.shape[1])
    idx = jnp.arange(L)
    causal = idx[:, None] >= idx[None, :]

    def step(carry, xs):
        C, n, m = carry
        qc, kc, vc, li, lf = xs
        b = jnp.cumsum(lf, axis=-1)
        logD = jnp.where(causal, b[..., :, None] - b[..., None, :] + li[..., None, :], -jnp.inf)
        inter = b + m[..., None]
        mt = jnp.maximum(inter, jnp.max(logD, axis=-1))
        s = jnp.einsum('bhid,bhjd->bhij', qc, kc) * jnp.exp(logD - mt[..., None])
        g_inter = jnp.exp(inter - mt)
        num = jnp.einsum('bhij,bhje->bhie', s, vc) + g_inter[..., None] * jnp.einsum('bhid,bhde->bhie', qc, C)
        den = jnp.sum(s, axis=-1) + g_inter * jnp.einsum('bhid,bhd->bhi', qc, n)
        h = num / jnp.maximum(jnp.abs(den), 1.0)[..., None]
        m_last = mt[..., -1]
        w_j = jnp.exp(b[..., -1:] - b + li - m_last[..., None])
        dC = jnp.exp(b[..., -1] + m - m_last)
        kw = kc * w_j[..., None]
        C = dC[..., None, None] * C + jnp.einsum('bhjd,bhje->bhde', kw, vc)
        n = dC[..., None] * n + jnp.sum(kw, axis=2)
        return (C, n, m_last), h

    xs = (_to_chunks(q, L), _to_chunks(k, L), _to_chunks(v, L), _to_chunks(log_i, L), _to_chunks(log_f, L))
    (C, n, m), h = lax.scan(step, (C0, n0, m0), xs)
    return _from_chunks(h), C, n, m


def _gated_delta(q, k, v, g, beta, S0):
    L = _chunk_len(q.shape[1])
    idx = jnp.arange(L)
    incl = idx[:, None] >= idx[None, :]
    strict = idx[:, None] > idx[None, :]
    eye = jnp.eye(L, dtype=jnp.float32)

    def step(S, xs):
        qc, kc, vc, gc, bc = xs
        G = jnp.cumsum(gc, axis=-1)
        decay = jnp.exp(jnp.where(incl, G[..., :, None] - G[..., None, :], -jnp.inf))
        A = jnp.where(strict, bc[..., :, None] * jnp.einsum('bhid,bhjd->bhij', kc, kc) * decay, 0.0)
        rhs = jnp.concatenate([vc * bc[..., None], kc * (bc * jnp.exp(G))[..., None]], axis=-1)
        sol = lax.linalg.triangular_solve(eye + A, rhs, left_side=True, lower=True)
        u, w = sol[..., :GDN_DV], sol[..., GDN_DV:]
        delta = u - jnp.einsum('bhik,bhkv->bhiv', w, S)
        o = (jnp.einsum('bhij,bhjv->bhiv', jnp.einsum('bhik,bhjk->bhij', qc, kc) * decay, delta)
             + jnp.exp(G)[..., None] * jnp.einsum('bhik,bhkv->bhiv', qc, S))
        G_last = G[..., -1]
        S = (jnp.exp(G_last)[..., None, None] * S
             + jnp.einsum('bhjk,bhjv->bhkv', kc * jnp.exp(G_last[..., None] - G)[..., None], delta))
        return S, o

    S, o = lax.scan(step, S0, tuple(_to_chunks(a, L) for a in (q, k, v, g, beta)))
    return _from_chunks(o), S


def _mixers(pm, ps, pos, st, conv_a_w, ret_norm_g, mlstm_b_i, mlstm_b_f, mlstm_norm_g,
            gdn_conv_w, gdn_a_log, gdn_dt_bias, gdn_norm_g):
    f32 = jnp.float32
    B, T, _ = pm.shape
    conv_a_buf, ret_S, m_C, m_n, m_m, g_buf, g_S = st

    def col(i, n=1):
        return pm[:, :, i * D_MIX:(i + n) * D_MIX]

    a_b, a_c, a_h, r_q, r_k, r_v, r_g, m_q, m_k, m_v, m_o = (col(i) for i in range(11))
    g_qkv, g_z = col(11, 3), col(14)
    m_i, m_f = ps[..., 0:4], ps[..., 4:8]
    g_beta, g_a = ps[..., 8:16], ps[..., 16:24]

    a_conv, conv_a_new = _causal_conv(a_c * a_h, conv_a_buf, conv_a_w)
    y_a = a_b * a_conv

    rq = _rope(_heads(r_q, RET_HEADS), pos)
    rk = _rope(_heads(r_k, RET_HEADS), pos) * RET_DK ** -0.5
    log_gamma = jnp.log1p(-jnp.exp2(-5.0 - jnp.arange(RET_HEADS, dtype=f32)))
    ro, ret_S_new = _retention(rq, rk, _heads(r_v, RET_HEADS), ret_S, log_gamma)
    y_r = jax.nn.silu(r_g) * _head_layernorm(ro, ret_norm_g)

    mk = _heads(m_k, MLSTM_HEADS) * MLSTM_DK ** -0.5
    log_i = m_i + mlstm_b_i
    log_f = jax.nn.log_sigmoid(m_f + mlstm_b_f)
    mh, C_new, n_new, m_new = _mlstm(_heads(m_q, MLSTM_HEADS), mk, _heads(m_v, MLSTM_HEADS), log_i, log_f,
                                     m_C, m_n, m_m)
    y_m = jax.nn.sigmoid(m_o) * _head_layernorm(mh, mlstm_norm_g)

    gqkv, g_buf_new = _causal_conv(g_qkv, g_buf, gdn_conv_w)
    gqkv = jax.nn.silu(gqkv)
    gq, gk, gv = jnp.split(gqkv, 3, axis=-1)
    gq = _l2norm(_heads(gq, GDN_HEADS)) * GDN_DK ** -0.5
    gk = _l2norm(_heads(gk, GDN_HEADS))
    beta = jax.nn.sigmoid(g_beta)
    g_log = -jnp.exp(gdn_a_log) * jax.nn.softplus(g_a + gdn_dt_bias)
    go, g_S_new = _gated_delta(gq, gk, _heads(gv, GDN_HEADS), g_log, beta, g_S)
    go = go * lax.rsqrt(jnp.mean(go * go, axis=-1, keepdims=True) + EPS) * gdn_norm_g
    y_g = go.reshape(B, T, D_MIX) * jax.nn.silu(g_z)

    y4 = jnp.stack([y_a, y_r, y_m, y_g]).astype(jnp.bfloat16).reshape(N_BRANCH, B * T, D_MIX)
    return y4, (conv_a_new, ret_S_new, C_new, n_new, m_new, g_buf_new, g_S_new)


def kernel(x_prompt, x_sample, state_conv_a, state_ret, state_mlstm_c, state_mlstm_n, state_mlstm_m, state_gdn_conv, state_gdn, norm1_g, norm2_g, norm_f_g, w_in, conv_a_w, ret_norm_g, mlstm_b_i, mlstm_b_f, mlstm_norm_g, gdn_conv_w, gdn_a_log, gdn_dt_bias, gdn_norm_g, w_branch, w_out, ffn_w_gu, ffn_w_down, moe_w_router, moe_w_gu, moe_w_down):
    bf16 = jnp.bfloat16
    f32 = jnp.float32
    bp, tp, d = x_prompt.shape
    bs, ts, _ = x_sample.shape
    mp, ms = bp * tp, bs * ts
    m_valid = mp + ms
    m_pad = ((m_valid + BM - 1) // BM) * BM
    x = jnp.concatenate([x_prompt.reshape(mp, d), x_sample.reshape(ms, d),
                         jnp.zeros((m_pad - m_valid, d), f32)], axis=0)
    pos_p = jnp.arange(tp)
    pos_s = PAST_LEN + jnp.arange(ts)
    sample_state = (state_conv_a, state_ret, state_mlstm_c, state_mlstm_n, state_mlstm_m, state_gdn_conv, state_gdn)
    new_p, new_s = [], []
    for l in range(DEPTH):
        wl = w_in[l]
        w_main = jnp.concatenate([wl[:, :_OFF_MI], wl[:, _OFF_GQKV:_OFF_GBETA], wl[:, _OFF_GATE:]], axis=1).astype(bf16)
        w_small = jnp.concatenate([wl[:, _OFF_MI:_OFF_GQKV], wl[:, _OFF_GBETA:_OFF_GATE],
                                   jnp.zeros((d, N_SMALL - 2 * MLSTM_HEADS - 2 * GDN_HEADS), f32)], axis=1).astype(bf16)
        h = _rmsnorm(x, norm1_g[l], bf16)
        proj_main = _mm(h, w_main, f32, BM, 1024, "in_proj")
        proj_small = _mm(h, w_small, f32, BM, N_SMALL, "in_proj_gates")
        mixer_w = (conv_a_w[l], ret_norm_g[l], mlstm_b_i[l], mlstm_b_f[l], mlstm_norm_g[l],
                   gdn_conv_w[l], gdn_a_log[l], gdn_dt_bias[l], gdn_norm_g[l])
        init_p = tuple(jnp.zeros((bp,) + s.shape[2:], f32) for s in sample_state)
        y4_p, st_p = _mixers(proj_main[:mp].reshape(bp, tp, N_MAIN), proj_small[:mp].reshape(bp, tp, N_SMALL),
                             pos_p, init_p, *mixer_w)
        y4_s, st_s = _mixers(proj_main[mp:m_valid].reshape(bs, ts, N_MAIN),
                             proj_small[mp:m_valid].reshape(bs, ts, N_SMALL),
                             pos_s, tuple(s[l] for s in sample_state), *mixer_w)
        new_p.append(st_p)
        new_s.append(st_s)
        y4 = jnp.concatenate([y4_p, y4_s, jnp.zeros((N_BRANCH, m_pad - m_valid, D_MIX), bf16)], axis=1)
        merged = _merge(y4, proj_main, w_branch[l].astype(bf16), BM, 512)
        x = _mm_res(merged, w_out[l].astype(bf16), x, BM, 1024, "out_proj")
        if l % 2 == 0:
            h2 = _rmsnorm(x, norm2_g[l], bf16)
            act = _ffn_gu(h2, ffn_w_gu[l // 2].astype(bf16), BM, 512)
            x = _ffn_down(act, ffn_w_down[l // 2].astype(bf16), x, BM, 1024, 2048)
        else:
            x = _moe(x, m_valid, norm2_g[l], moe_w_router[l // 2], moe_w_gu[l // 2].astype(bf16),
                     moe_w_down[l // 2].astype(bf16))
    y = _rmsnorm(x, norm_f_g, f32)
    y_prompt = y[:mp].reshape(bp, tp, d)
    y_sample = y[mp:m_valid].reshape(bs, ts, d)
    outs = [y_prompt, y_sample]
    for i in range(7):
        outs.append(jnp.stack([new_p[l][i] for l in range(DEPTH)]))
        outs.append(jnp.stack([new_s[l][i] for l in range(DEPTH)]))
    return tuple(outs)
```

```python
import functools

import jax
import jax.numpy as jnp
from jax import lax
from jax.experimental import pallas as pl
from jax.experimental.pallas import tpu as pltpu

D_MODEL = 4096
DEPTH = 2
N_BRANCH = 4
D_MIX = D_MODEL // 4
CONV_A_WIDTH = 3
RET_HEADS = 4
RET_DK = D_MIX // RET_HEADS
MLSTM_HEADS = 4
MLSTM_DK = D_MIX // MLSTM_HEADS
GDN_HEADS = 8
GDN_DK = D_MIX // GDN_HEADS
GDN_CONV = 4
ROPE_BASE = 10000.0
N_EXPERTS = 8
TOP_K = 2
EPS = 1e-6
PAST_LEN = 2048

_OFF_MI = 11 * D_MIX
_OFF_GQKV = _OFF_MI + 2 * MLSTM_HEADS
_OFF_GBETA = _OFF_GQKV + 4 * D_MIX
_OFF_GATE = _OFF_GBETA + 2 * GDN_HEADS
N_MAIN = 15 * D_MIX + N_BRANCH * D_MODEL
N_SMALL = 128
_PS_MI, _PS_MF, _PS_BETA, _PS_A = 0, MLSTM_HEADS, 2 * MLSTM_HEADS, 2 * MLSTM_HEADS + GDN_HEADS

V7X_VMEM_LIMIT = 56 * 1024 * 1024
BM = 768
BM_MOE = 512
MIX_CHUNK = 256
CONV_ROWS = 512
GDN_HEADS_PER_STEP = 2

_BF16 = jnp.bfloat16
_F32 = jnp.float32


def _cparams(sem):
    return pltpu.CompilerParams(dimension_semantics=sem, vmem_limit_bytes=V7X_VMEM_LIMIT)


def _rmsnorm_kernel(x_ref, g_ref, o_ref):
    x = x_ref[...]
    y = x * lax.rsqrt(jnp.mean(x * x, axis=-1, keepdims=True) + EPS)
    o_ref[...] = (y * g_ref[...]).astype(o_ref.dtype)


def _rmsnorm(x, g, out_dtype, bm=256):
    m, d = x.shape
    return pl.pallas_call(
        _rmsnorm_kernel,
        grid=(m // bm,),
        in_specs=[pl.BlockSpec((bm, d), lambda i: (i, 0)), pl.BlockSpec((1, d), lambda i: (0, 0))],
        out_specs=pl.BlockSpec((bm, d), lambda i: (i, 0)),
        out_shape=jax.ShapeDtypeStruct((m, d), out_dtype),
        compiler_params=_cparams(("parallel",)),
        name="rmsnorm",
    )(x, g.reshape(1, d))


def _mm_kernel(x_ref, w_ref, o_ref):
    o_ref[...] = jnp.dot(x_ref[...], w_ref[...], preferred_element_type=_F32).astype(o_ref.dtype)


def _mm(x, w, out_dtype, bm, bn, name):
    m, k = x.shape
    n = w.shape[1]
    return pl.pallas_call(
        _mm_kernel,
        grid=(m // bm, n // bn),
        in_specs=[pl.BlockSpec((bm, k), lambda i, j: (i, 0)), pl.BlockSpec((k, bn), lambda i, j: (0, j))],
        out_specs=pl.BlockSpec((bm, bn), lambda i, j: (i, j)),
        out_shape=jax.ShapeDtypeStruct((m, n), out_dtype),
        compiler_params=_cparams(("parallel", "arbitrary")),
        name=name,
    )(x, w)


def _mm_res_kernel(x_ref, w_ref, r_ref, o_ref):
    o_ref[...] = r_ref[...] + jnp.dot(x_ref[...], w_ref[...], preferred_element_type=_F32)


def _mm_res(x, w, res, bm, bn, name):
    m, k = x.shape
    n = w.shape[1]
    return pl.pallas_call(
        _mm_res_kernel,
        grid=(m // bm, n // bn),
        in_specs=[pl.BlockSpec((bm, k), lambda i, j: (i, 0)), pl.BlockSpec((k, bn), lambda i, j: (0, j)),
                  pl.BlockSpec((bm, bn), lambda i, j: (i, j))],
        out_specs=pl.BlockSpec((bm, bn), lambda i, j: (i, j)),
        out_shape=jax.ShapeDtypeStruct((m, n), _F32),
        compiler_params=_cparams(("parallel", "arbitrary")),
        name=name,
    )(x, w, res)


def _merge_kernel(ya_ref, yr_ref, ym_ref, yg_ref, g0_ref, g1_ref, g2_ref, g3_ref, w_ref, o_ref):
    acc = None
    for b, (y_ref, g_ref) in enumerate(((ya_ref, g0_ref), (yr_ref, g1_ref), (ym_ref, g2_ref), (yg_ref, g3_ref))):
        t = jax.nn.sigmoid(g_ref[...]) * jnp.dot(y_ref[...], w_ref[b], preferred_element_type=_F32)
        acc = t if acc is None else acc + t
    o_ref[...] = acc.astype(o_ref.dtype)


def _merge(ys, proj_main, w_branch, bm, bn):
    m, dm = ys[0].shape
    n = w_branch.shape[2]
    gate_blk0 = (15 * D_MIX) // bn
    per_branch = n // bn

    def gate_spec(b):
        return pl.BlockSpec((bm, bn), lambda i, j: (i, gate_blk0 + b * per_branch + j))

    y_spec = pl.BlockSpec((bm, dm), lambda i, j: (i, 0))
    return pl.pallas_call(
        _merge_kernel,
        grid=(m // bm, n // bn),
        in_specs=[y_spec, y_spec, y_spec, y_spec, gate_spec(0), gate_spec(1), gate_spec(2), gate_spec(3),
                  pl.BlockSpec((N_BRANCH, dm, bn), lambda i, j: (0, 0, j))],
        out_specs=pl.BlockSpec((bm, bn), lambda i, j: (i, j)),
        out_shape=jax.ShapeDtypeStruct((m, n), _BF16),
        compiler_params=_cparams(("parallel", "arbitrary")),
        name="branch_merge",
    )(*ys, proj_main, proj_main, proj_main, proj_main, w_branch)


def _gu_kernel(x_ref, wg_ref, wu_ref, o_ref):
    x = x_ref[...]
    g = jnp.dot(x, wg_ref[...], preferred_element_type=_F32)
    u = jnp.dot(x, wu_ref[...], preferred_element_type=_F32)
    o_ref[...] = (g * jax.nn.sigmoid(g) * u).astype(o_ref.dtype)


def _ffn_gu(x, w_gu, bm, bn):
    m, k = x.shape
    f = w_gu.shape[1] // 2
    nb = f // bn
    return pl.pallas_call(
        _gu_kernel,
        grid=(m // bm, nb),
        in_specs=[pl.BlockSpec((bm, k), lambda i, j: (i, 0)),
                  pl.BlockSpec((k, bn), lambda i, j: (0, j)),
                  pl.BlockSpec((k, bn), lambda i, j: (0, nb + j))],
        out_specs=pl.BlockSpec((bm, bn), lambda i, j: (i, j)),
        out_shape=jax.ShapeDtypeStruct((m, f), _BF16),
        compiler_params=_cparams(("parallel", "arbitrary")),
        name="ffn_gate_up",
    )(x, w_gu, w_gu)


def _down_kernel(x_ref, w_ref, r_ref, o_ref, acc_ref):
    k = pl.program_id(2)

    @pl.when(k == 0)
    def _():
        acc_ref[...] = r_ref[...]

    acc_ref[...] += jnp.dot(x_ref[...], w_ref[...], preferred_element_type=_F32)

    @pl.when(k == pl.num_programs(2) - 1)
    def _():
        o_ref[...] = acc_ref[...]


def _ffn_down(act, w_down, res, bm, bn, bk):
    m, f = act.shape
    n = w_down.shape[1]
    return pl.pallas_call(
        _down_kernel,
        grid=(m // bm, n // bn, f // bk),
        in_specs=[pl.BlockSpec((bm, bk), lambda i, j, k: (i, k)),
                  pl.BlockSpec((bk, bn), lambda i, j, k: (k, j)),
                  pl.BlockSpec((bm, bn), lambda i, j, k: (i, j))],
        out_specs=pl.BlockSpec((bm, bn), lambda i, j, k: (i, j)),
        out_shape=jax.ShapeDtypeStruct((m, n), _F32),
        scratch_shapes=[pltpu.VMEM((bm, bn), _F32)],
        compiler_params=_cparams(("parallel", "arbitrary", "arbitrary")),
        name="ffn_down",
    )(act, w_down, res)


def _router_kernel(x_ref, g_ref, w_ref, h_ref, l_ref):
    x = x_ref[...]
    h = x * lax.rsqrt(jnp.mean(x * x, axis=-1, keepdims=True) + EPS) * g_ref[...]
    h_ref[...] = h.astype(h_ref.dtype)
    l_ref[...] = jnp.dot(h, w_ref[...], preferred_element_type=_F32, precision=lax.Precision.HIGHEST)


def _norm_router(x, g, w_router_pad, bm=256):
    m, d = x.shape
    ne = w_router_pad.shape[1]
    return pl.pallas_call(
        _router_kernel,
        grid=(m // bm,),
        in_specs=[pl.BlockSpec((bm, d), lambda i: (i, 0)), pl.BlockSpec((1, d), lambda i: (0, 0)),
                  pl.BlockSpec((d, ne), lambda i: (0, 0))],
        out_specs=[pl.BlockSpec((bm, d), lambda i: (i, 0)), pl.BlockSpec((bm, ne), lambda i: (i, 0))],
        out_shape=[jax.ShapeDtypeStruct((m, d), _BF16), jax.ShapeDtypeStruct((m, ne), _F32)],
        compiler_params=_cparams(("parallel",)),
        name="norm_router",
    )(x, g.reshape(1, d), w_router_pad)


def _moe_gu_kernel(te_ref, nt_ref, x_ref, wg_ref, wu_ref, o_ref):
    @pl.when(pl.program_id(0) < nt_ref[0])
    def _():
        x = x_ref[...]
        g = jnp.dot(x, wg_ref[...], preferred_element_type=_F32)
        u = jnp.dot(x, wu_ref[...], preferred_element_type=_F32)
        o_ref[...] = (g * jax.nn.sigmoid(g) * u).astype(o_ref.dtype)

    @pl.when(pl.program_id(0) >= nt_ref[0])
    def _():
        o_ref[...] = jnp.zeros_like(o_ref)


def _moe_gu(tile_expert, n_tiles, xs, w_gu, bm, bn):
    m, k = xs.shape
    f = w_gu.shape[2] // 2
    nb = f // bn
    grid_spec = pltpu.PrefetchScalarGridSpec(
        num_scalar_prefetch=2,
        grid=(m // bm, nb),
        in_specs=[pl.BlockSpec((bm, k), lambda t, j, te, nt: (t, 0)),
                  pl.BlockSpec((None, k, bn), lambda t, j, te, nt: (te[t], 0, j)),
                  pl.BlockSpec((None, k, bn), lambda t, j, te, nt: (te[t], 0, nb + j))],
        out_specs=pl.BlockSpec((bm, bn), lambda t, j, te, nt: (t, j)),
    )
    return pl.pallas_call(
        _moe_gu_kernel,
        grid_spec=grid_spec,
        out_shape=jax.ShapeDtypeStruct((m, f), _BF16),
        compiler_params=_cparams(("parallel", "arbitrary")),
        name="moe_gate_up",
    )(tile_expert, n_tiles, xs, w_gu, w_gu)


def _moe_down_kernel(te_ref, nt_ref, x_ref, w_ref, o_ref, acc_ref):
    k = pl.program_id(2)

    @pl.when(k == 0)
    def _():
        acc_ref[...] = jnp.zeros_like(acc_ref)

    @pl.when(pl.program_id(0) < nt_ref[0])
    def _():
        acc_ref[...] += jnp.dot(x_ref[...], w_ref[...], preferred_element_type=_F32)

    @pl.when(k == pl.num_programs(2) - 1)
    def _():
        o_ref[...] = acc_ref[...]


def _moe_down(tile_expert, n_tiles, act, w_down, bm, bn, bk):
    m, f = act.shape
    n = w_down.shape[2]
    grid_spec = pltpu.PrefetchScalarGridSpec(
        num_scalar_prefetch=2,
        grid=(m // bm, n // bn, f // bk),
        in_specs=[pl.BlockSpec((bm, bk), lambda t, j, k, te, nt: (t, k)),
                  pl.BlockSpec((None, bk, bn), lambda t, j, k, te, nt: (te[t], k, j))],
        out_specs=pl.BlockSpec((bm, bn), lambda t, j, k, te, nt: (t, j)),
        scratch_shapes=[pltpu.VMEM((bm, bn), _F32)],
    )
    return pl.pallas_call(
        _moe_down_kernel,
        grid_spec=grid_spec,
        out_shape=jax.ShapeDtypeStruct((m, n), _F32),
        compiler_params=_cparams(("parallel", "arbitrary", "arbitrary")),
        name="moe_down",
    )(tile_expert, n_tiles, act, w_down)


def _moe(x, norm_g, w_router, w_gu, w_down):
    m, d = x.shape
    bm = BM_MOE
    wr = jnp.zeros((d, N_SMALL), _F32).at[:, :N_EXPERTS].set(w_router)
    h, logits = _norm_router(x, norm_g, wr)
    top_v, top_i = lax.top_k(logits[:, :N_EXPERTS], TOP_K)
    weights = jax.nn.softmax(top_v, axis=-1)
    n_pairs = m * TOP_K
    e_flat = top_i.reshape(n_pairs)
    onehot = (e_flat[:, None] == jnp.arange(N_EXPERTS)[None, :]).astype(jnp.int32)
    csum = jnp.cumsum(onehot, axis=0)
    counts = csum[-1]
    rank = jnp.sum(csum * onehot, axis=1) - 1
    padded = ((counts + bm - 1) // bm) * bm
    ends = jnp.cumsum(padded)
    start = ends - padded
    dest = start[e_flat] + rank
    n_tiles_max = n_pairs // bm + N_EXPERTS
    n_slots = n_tiles_max * bm
    n_tiles = (ends[-1] // bm).astype(jnp.int32)
    src_tok = jnp.zeros((n_slots,), jnp.int32).at[dest].set(jnp.arange(n_pairs, dtype=jnp.int32) // TOP_K)
    tile_start = jnp.arange(n_tiles_max, dtype=jnp.int32) * bm
    tile_expert = jnp.sum((tile_start[:, None] >= ends[None, :]).astype(jnp.int32), axis=1)
    last_e = jnp.max(jnp.where(counts > 0, jnp.arange(N_EXPERTS), 0)).astype(jnp.int32)
    tile_expert = jnp.minimum(tile_expert, last_e).astype(jnp.int32)
    xs = jnp.take(h, src_tok, axis=0)
    act = _moe_gu(tile_expert, n_tiles.reshape(1), xs, w_gu, bm, 512)
    ys = _moe_down(tile_expert, n_tiles.reshape(1), act, w_down, bm, 1024, 2048)
    yp = jnp.take(ys, dest, axis=0).reshape(m, TOP_K, d)
    return x + weights[:, 0:1] * yp[:, 0] + weights[:, 1:2] * yp[:, 1]


def _dot(a, b):
    return jnp.dot(a.astype(_BF16), b.astype(_BF16), preferred_element_type=_F32)


def _dot_nt(a, b):
    return lax.dot_general(a.astype(_BF16), b.astype(_BF16), (((1,), (1,)), ((), ())), preferred_element_type=_F32)


def _dot_tn(a, b):
    return lax.dot_general(a.astype(_BF16), b.astype(_BF16), (((0,), (0,)), ((), ())), preferred_element_type=_F32)


def _split2(x):
    hi = x.astype(_BF16)
    return hi, (x - hi.astype(_F32)).astype(_BF16)


def _dot_f32(a, b):
    ah, al = _split2(a)
    bh, bl = _split2(b)
    return (jnp.dot(ah, bh, preferred_element_type=_F32)
            + (jnp.dot(ah, bl, preferred_element_type=_F32) + jnp.dot(al, bh, preferred_element_type=_F32)))


def _dot_cum(tril, x):
    t = tril.astype(_BF16)
    hi = x.astype(_BF16)
    r = x - hi.astype(_F32)
    mid = r.astype(_BF16)
    lo = (r - mid.astype(_F32)).astype(_BF16)
    return (jnp.dot(t, hi, preferred_element_type=_F32)
            + (jnp.dot(t, mid, preferred_element_type=_F32) + jnp.dot(t, lo, preferred_element_type=_F32)))


def _col(x, idx):
    lane = lax.broadcasted_iota(jnp.int32, x.shape, 1)
    return jnp.sum(jnp.where(lane == idx, x, 0.0), axis=-1, keepdims=True)


def _iotas(n):
    return lax.broadcasted_iota(jnp.int32, (n, n), 0), lax.broadcasted_iota(jnp.int32, (n, n), 1)


def _softplus(x):
    return jnp.maximum(x, 0.0) + jnp.log1p(jnp.exp(-jnp.abs(x)))


def _shifted_rows(u, k, carry):
    r = pltpu.roll(u, k, axis=0)
    row = lax.broadcasted_iota(jnp.int32, u.shape, 0)
    nc = carry.shape[0]
    for i in range(k):
        r = jnp.where(row == i, carry[nc - k + i:nc - k + i + 1, :], r)
    return r


def _causal_conv_rows(u, w, carry):
    kw = w.shape[0]
    acc = w[kw - 1:kw, :] * u
    for k in range(1, kw):
        acc = acc + w[kw - 1 - k:kw - k, :] * _shifted_rows(u, k, carry)
    return acc


def _conv_a_kernel(b_ref, c_ref, h_ref, w_ref, hist_ref, y_ref, st_ref, carry_ref):
    @pl.when(pl.program_id(2) == 0)
    def _():
        carry_ref[...] = hist_ref[...]

    u = c_ref[...] * h_ref[...]
    acc = _causal_conv_rows(u, w_ref[...], carry_ref[...])
    y_ref[...] = (b_ref[...] * acc).astype(y_ref.dtype)
    tail = u[u.shape[0] - (CONV_A_WIDTH - 1):, :]
    carry_ref[...] = tail
    st_ref[...] = tail


def _conv_g_kernel(x_ref, w_ref, hist_ref, y_ref, st_ref, carry_ref):
    @pl.when(pl.program_id(2) == 0)
    def _():
        carry_ref[...] = hist_ref[...]

    u = x_ref[...]
    acc = _causal_conv_rows(u, w_ref[...], carry_ref[...])
    y_ref[...] = acc * jax.nn.sigmoid(acc)
    tail = u[u.shape[0] - (GDN_CONV - 1):, :]
    carry_ref[...] = tail
    st_ref[...] = tail


def _conv_calls(proj_main, row0, nb, t, tb, conv_a_w, hist_a, gdn_conv_w, hist_g):
    nt = t // tb
    r0 = row0 // tb
    cb = D_MIX

    def rows(c0):
        return pl.BlockSpec((tb, cb), lambda b, j, i: (r0 + b * nt + i, c0 + j))

    ka, kg = CONV_A_WIDTH, GDN_CONV
    y_a, st_a = pl.pallas_call(
        _conv_a_kernel,
        grid=(nb, 1, nt),
        in_specs=[rows(0), rows(1), rows(2),
                  pl.BlockSpec((ka, cb), lambda b, j, i: (0, j)),
                  pl.BlockSpec((None, ka - 1, cb), lambda b, j, i: (b, 0, j))],
        out_specs=[pl.BlockSpec((tb, cb), lambda b, j, i: (b * nt + i, j)),
                   pl.BlockSpec((None, ka - 1, cb), lambda b, j, i: (b, 0, j))],
        out_shape=[jax.ShapeDtypeStruct((nb * t, D_MIX), _BF16),
                   jax.ShapeDtypeStruct((nb, ka - 1, D_MIX), _F32)],
        scratch_shapes=[pltpu.VMEM((ka - 1, cb), _F32)],
        compiler_params=_cparams(("parallel", "parallel", "arbitrary")),
        name="conv_a",
    )(proj_main, proj_main, proj_main, conv_a_w, hist_a)
    gqkv, st_g = pl.pallas_call(
        _conv_g_kernel,
        grid=(nb, 3, nt),
        in_specs=[rows(11),
                  pl.BlockSpec((kg, cb), lambda b, j, i: (0, j)),
                  pl.BlockSpec((None, kg - 1, cb), lambda b, j, i: (b, 0, j))],
        out_specs=[pl.BlockSpec((tb, cb), lambda b, j, i: (b * nt + i, j)),
                   pl.BlockSpec((None, kg - 1, cb), lambda b, j, i: (b, 0, j))],
        out_shape=[jax.ShapeDtypeStruct((nb * t, 3 * D_MIX), _F32),
                   jax.ShapeDtypeStruct((nb, kg - 1, 3 * D_MIX), _F32)],
        scratch_shapes=[pltpu.VMEM((kg - 1, cb), _F32)],
        compiler_params=_cparams(("parallel", "parallel", "arbitrary")),
        name="conv_gdn",
    )(proj_main, gdn_conv_w, hist_g)
    return y_a, st_a, gqkv, st_g


def _head_norm_gate(o, gain, gate):
    oc = o - jnp.mean(o, axis=-1, keepdims=True)
    y = oc * lax.rsqrt(jnp.mean(oc * oc, axis=-1, keepdims=True) + EPS)
    return gate * (y * gain)


def _ret_kernel(q_ref, k_ref, v_ref, g_ref, cos_ref, sin_ref, ng_ref, lg_ref, s0_ref, y_ref, s_ref):
    L = q_ref.shape[0]

    @pl.when(pl.program_id(2) == 0)
    def _():
        s_ref[...] = s0_ref[...]

    lg = lg_ref[...][:, :1]
    cos, sin = cos_ref[...], sin_ref[...]
    half = RET_DK // 2

    def rope(x):
        x1, x2 = x[:, :half], x[:, half:]
        return jnp.concatenate([x1 * cos - x2 * sin, x1 * sin + x2 * cos], axis=-1)

    rq = rope(q_ref[...])
    rk = rope(k_ref[...]) * RET_DK ** -0.5
    v = v_ref[...]
    row, col = _iotas(L)
    intra = jnp.where(row >= col, jnp.exp((row - col).astype(_F32) * lg), 0.0)
    ridx = lax.broadcasted_iota(jnp.int32, (L, 1), 0).astype(_F32)
    inter = jnp.exp((ridx + 1.0) * lg)
    w_k = jnp.exp((L - 1.0 - ridx) * lg)
    dec = jnp.exp(L * lg)
    S = s_ref[...]
    s = _dot_nt(rq, rk) * intra
    o = _dot(s, v) + inter * _dot(rq, S)
    s_ref[...] = dec * S + _dot_tn(rk * w_k, v)
    g = g_ref[...]
    y_ref[...] = _head_norm_gate(o, ng_ref[...], g * jax.nn.sigmoid(g)).astype(y_ref.dtype)


def _mlstm_kernel(q_ref, k_ref, v_ref, og_ref, ps_ref, bias_ref, ng_ref, c0_ref, n0_ref, m0_ref,
                  y_ref, c_ref, n_ref, m_ref):
    L = q_ref.shape[0]
    h = pl.program_id(1)

    @pl.when(pl.program_id(2) == 0)
    def _():
        c_ref[...] = c0_ref[...]
        n_ref[...] = n0_ref[...]
        m_ref[...] = m0_ref[...]

    z = ps_ref[...] + bias_ref[...]
    li = _col(z, _PS_MI + h)
    fp = _col(z, _PS_MF + h)
    lf = jnp.minimum(fp, 0.0) - jnp.log1p(jnp.exp(-jnp.abs(fp)))
    row, col = _iotas(L)
    causal = row >= col
    tril = causal.astype(_F32)
    logd = _dot_cum(tril, jnp.where(row > col, lf, jnp.where(row == col, li, 0.0)))
    b = _dot_cum(tril, jnp.broadcast_to(lf, (L, N_SMALL)))[:, :1]
    m_prev = m_ref[...][:, :1]
    inter = b + m_prev
    mt = jnp.maximum(inter, jnp.max(jnp.where(causal, logd, -jnp.inf), axis=-1, keepdims=True))
    dmat = jnp.where(causal, jnp.exp(logd - mt), 0.0)
    q = q_ref[...]
    k = k_ref[...] * MLSTM_DK ** -0.5
    v = v_ref[...]
    s = _dot_nt(q, k) * dmat
    g_inter = jnp.exp(inter - mt)
    C = c_ref[...]
    n = n_ref[...]
    num = _dot(s, v) + g_inter * _dot(q, C)
    den = jnp.sum(s, axis=-1, keepdims=True) + g_inter * jnp.sum(q * n, axis=-1, keepdims=True)
    hh = num / jnp.maximum(jnp.abs(den), 1.0)
    m_last = mt[L - 1:L, :]
    b_last = b[L - 1:L, :]
    kw = k * jnp.exp(b_last - b + li - m_last)
    d_c = jnp.exp(b_last + m_prev - m_last)
    c_ref[...] = d_c * C + _dot_tn(kw, v)
    n_ref[...] = d_c * n + jnp.sum(kw, axis=0, keepdims=True)
    m_ref[...] = jnp.broadcast_to(m_last, m_ref.shape)
    y_ref[...] = _head_norm_gate(hh, ng_ref[...], jax.nn.sigmoid(og_ref[...])).astype(y_ref.dtype)


def _unit_lower_inverse(a, n):
    row, col = _iotas(n)
    eye = (row == col).astype(_F32)

    def same_block(shift):
        return jnp.right_shift(row, shift) == jnp.right_shift(col, shift)

    ad = jnp.where(same_block(3), a, 0.0)
    a2 = _dot_f32(ad, ad)
    a4 = _dot_f32(a2, a2)
    t = _dot_f32(_dot_f32(eye - ad, eye + a2), eye + a4)
    shift = 3
    while (1 << shift) < n:
        off = jnp.logical_and(same_block(shift + 1), jnp.logical_not(same_block(shift)))
        t = t - _dot_f32(_dot_f32(t, jnp.where(off, a, 0.0)), t)
        shift += 1
    return t


def _gdn_kernel(q_ref, k_ref, v_ref, z_ref, ps_ref, alog_ref, dtb_ref, ng_ref, s0_ref, y_ref, s_ref):
    L = q_ref.shape[0]

    @pl.when(pl.program_id(2) == 0)
    def _():
        s_ref[...] = s0_ref[...]

    ps = ps_ref[...]
    ps_a = ps + dtb_ref[...]
    row, col = _iotas(L)
    incl = row >= col
    strict = row > col
    tril = incl.astype(_F32)

    def l2n(x):
        return x * lax.rsqrt(jnp.sum(x * x, axis=-1, keepdims=True) + EPS)

    for j in range(GDN_HEADS_PER_STEP):
        h = pl.program_id(1) * GDN_HEADS_PER_STEP + j
        lanes = slice(j * GDN_DK, (j + 1) * GDN_DK)
        beta = jax.nn.sigmoid(_col(ps, _PS_BETA + h))
        neg_a = -jnp.exp(_col(alog_ref[...], _PS_A + h))
        g = neg_a * _softplus(_col(ps_a, _PS_A + h))
        q = l2n(q_ref[:, lanes]) * GDN_DK ** -0.5
        k = l2n(k_ref[:, lanes])
        v = v_ref[:, lanes]
        gd = _dot_cum(tril, jnp.where(strict, g, 0.0))
        G = _dot_cum(tril, jnp.broadcast_to(g, (L, N_SMALL)))[:, :1]
        decay = jnp.where(incl, jnp.exp(gd), 0.0)
        a = jnp.where(strict, beta * _dot_nt(k, k) * decay, 0.0)
        e_g = jnp.exp(G)
        rhs = jnp.concatenate([v * beta, k * (beta * e_g)], axis=-1)
        sol = _dot_f32(_unit_lower_inverse(a, L), rhs)
        u, w = sol[:, :GDN_DK], sol[:, GDN_DK:]
        S = s_ref[j]
        delta = u - _dot(w, S)
        o = _dot(_dot_nt(q, k) * decay, delta) + e_g * _dot(q, S)
        g_last = G[L - 1:L, :]
        s_ref[j] = jnp.exp(g_last) * S + _dot_tn(k * jnp.exp(g_last - G), delta)
        go = o * lax.rsqrt(jnp.mean(o * o, axis=-1, keepdims=True) + EPS) * ng_ref[...]
        z = z_ref[:, lanes]
        y_ref[:, lanes] = (go * (z * jax.nn.sigmoid(z))).astype(y_ref.dtype)


def _rope_tables(pos):
    half = RET_DK // 2
    inv = ROPE_BASE ** (-jnp.arange(half, dtype=_F32) / half)
    ang = pos.astype(_F32)[:, None] * inv[None, :]
    return jnp.cos(ang), jnp.sin(ang)


def _lane_row(vals, off):
    return jnp.zeros((1, N_SMALL), _F32).at[0, off:off + vals.shape[0]].set(vals)


def _mixers(proj_main, proj_small, row0, nb, t, L, tb, pos, st, conv_a_w, ret_norm_g, mlstm_b_i, mlstm_b_f,
            mlstm_norm_g, gdn_conv_w, gdn_a_log, gdn_dt_bias, gdn_norm_g):
    conv_a_buf, ret_S, m_C, m_n, m_m, g_buf, g_S = st
    nc = t // L
    r0 = row0 // L
    sem = ("parallel", "parallel", "arbitrary")
    y_a, st_a, gqkv, st_g = _conv_calls(proj_main, row0, nb, t, tb, conv_a_w, conv_a_buf, gdn_conv_w, g_buf)

    def main_cols(group, width, heads):
        return pl.BlockSpec((L, width), lambda b, h, c: (r0 + b * nc + c, group * heads + h))

    def out_cols(width):
        return pl.BlockSpec((L, width), lambda b, h, c: (b * nc + c, h))

    def state_spec(*dims):
        return pl.BlockSpec((None, None) + dims, lambda b, h, c: (b, h) + (0,) * len(dims))

    def gain_spec(width):
        return pl.BlockSpec((1, width), lambda b, h, c: (0, h))

    small_spec = pl.BlockSpec((L, N_SMALL), lambda b, h, c: (r0 + b * nc + c, 0))
    lane_spec = pl.BlockSpec((1, N_SMALL), lambda b, h, c: (0, 0))

    cos, sin = _rope_tables(pos)
    log_gamma = jnp.log1p(-jnp.exp2(-5.0 - jnp.arange(RET_HEADS, dtype=_F32)))
    lg_tab = jnp.broadcast_to(log_gamma[:, None, None], (RET_HEADS, 1, N_SMALL))
    rope_spec = pl.BlockSpec((L, RET_DK // 2), lambda b, h, c: (c, 0))
    y_r, ret_new = pl.pallas_call(
        _ret_kernel,
        grid=(nb, RET_HEADS, nc),
        in_specs=[main_cols(3, RET_DK, RET_HEADS), main_cols(4, RET_DK, RET_HEADS),
                  main_cols(5, RET_DK, RET_HEADS), main_cols(6, RET_DK, RET_HEADS),
                  rope_spec, rope_spec, gain_spec(RET_DK),
                  pl.BlockSpec((None, 1, N_SMALL), lambda b, h, c: (h, 0, 0)),
                  state_spec(RET_DK, RET_DK)],
        out_specs=[out_cols(RET_DK), state_spec(RET_DK, RET_DK)],
        out_shape=[jax.ShapeDtypeStruct((nb * t, D_MIX), _BF16),
                   jax.ShapeDtypeStruct((nb, RET_HEADS, RET_DK, RET_DK), _F32)],
        compiler_params=_cparams(sem),
        name="retention",
    )(proj_main, proj_main, proj_main, proj_main, cos, sin, ret_norm_g.reshape(1, D_MIX), lg_tab, ret_S)

    bias_row = _lane_row(jnp.concatenate([mlstm_b_i, mlstm_b_f]), _PS_MI)
    y_m, c_new, n_new, m_new = pl.pallas_call(
        _mlstm_kernel,
        grid=(nb, MLSTM_HEADS, nc),
        in_specs=[main_cols(7, MLSTM_DK, MLSTM_HEADS), main_cols(8, MLSTM_DK, MLSTM_HEADS),
                  main_cols(9, MLSTM_DK, MLSTM_HEADS), main_cols(10, MLSTM_DK, MLSTM_HEADS),
                  small_spec, lane_spec, gain_spec(MLSTM_DK),
                  state_spec(MLSTM_DK, MLSTM_DK), state_spec(1, MLSTM_DK), state_spec(1, N_SMALL)],
        out_specs=[out_cols(MLSTM_DK), state_spec(MLSTM_DK, MLSTM_DK), state_spec(1, MLSTM_DK),
                   state_spec(1, N_SMALL)],
        out_shape=[jax.ShapeDtypeStruct((nb * t, D_MIX), _BF16),
                   jax.ShapeDtypeStruct((nb, MLSTM_HEADS, MLSTM_DK, MLSTM_DK), _F32),
                   jax.ShapeDtypeStruct((nb, MLSTM_HEADS, 1, MLSTM_DK), _F32),
                   jax.ShapeDtypeStruct((nb, MLSTM_HEADS, 1, N_SMALL), _F32)],
        compiler_params=_cparams(sem),
        name="mlstm",
    )(proj_main, proj_main, proj_main, proj_main, proj_small, bias_row, mlstm_norm_g.reshape(1, D_MIX),
      m_C, m_n.reshape(nb, MLSTM_HEADS, 1, MLSTM_DK),
      jnp.broadcast_to(m_m[:, :, None, None], (nb, MLSTM_HEADS, 1, N_SMALL)))

    hps = GDN_HEADS_PER_STEP
    n_hg = GDN_HEADS // hps

    def conv_cols(group):
        return pl.BlockSpec((L, hps * GDN_DK), lambda b, h, c: (b * nc + c, group * n_hg + h))

    gdn_state = pl.BlockSpec((None, hps, GDN_DK, GDN_DK), lambda b, h, c: (b, h, 0, 0))
    y_g, gdn_new = pl.pallas_call(
        _gdn_kernel,
        grid=(nb, n_hg, nc),
        in_specs=[conv_cols(0), conv_cols(1), conv_cols(2), main_cols(14, hps * GDN_DK, n_hg),
                  small_spec, lane_spec, lane_spec,
                  pl.BlockSpec((1, GDN_DK), lambda b, h, c: (0, 0)),
                  gdn_state],
        out_specs=[out_cols(hps * GDN_DK), gdn_state],
        out_shape=[jax.ShapeDtypeStruct((nb * t, D_MIX), _BF16),
                   jax.ShapeDtypeStruct((nb, GDN_HEADS, GDN_DK, GDN_DK), _F32)],
        compiler_params=_cparams(sem),
        name="gated_delta",
    )(gqkv, gqkv, gqkv, proj_main, proj_small, _lane_row(gdn_a_log, _PS_A), _lane_row(gdn_dt_bias, _PS_A),
      gdn_norm_g.reshape(1, GDN_DK), g_S)

    new_state = (st_a, ret_new, c_new, n_new.reshape(nb, MLSTM_HEADS, MLSTM_DK), m_new[:, :, 0, 0], st_g, gdn_new)
    return (y_a, y_r, y_m, y_g), new_state


def kernel(x_prompt, x_sample, state_conv_a, state_ret, state_mlstm_c, state_mlstm_n, state_mlstm_m, state_gdn_conv, state_gdn, norm1_g, norm2_g, norm_f_g, w_in, conv_a_w, ret_norm_g, mlstm_b_i, mlstm_b_f, mlstm_norm_g, gdn_conv_w, gdn_a_log, gdn_dt_bias, gdn_norm_g, w_branch, w_out, ffn_w_gu, ffn_w_down, moe_w_router, moe_w_gu, moe_w_down):
    bp, tp, d = x_prompt.shape
    bs, ts, _ = x_sample.shape
    mp, ms = bp * tp, bs * ts
    m = mp + ms
    x = jnp.concatenate([x_prompt.reshape(mp, d), x_sample.reshape(ms, d)], axis=0)
    pos_p = jnp.arange(tp)
    pos_s = PAST_LEN + jnp.arange(ts)
    sample_state = (state_conv_a, state_ret, state_mlstm_c, state_mlstm_n, state_mlstm_m, state_gdn_conv, state_gdn)
    chunk_p = MIX_CHUNK if tp % MIX_CHUNK == 0 else tp
    conv_rows_p = CONV_ROWS if tp % CONV_ROWS == 0 else tp
    new_p, new_s = [], []
    for l in range(DEPTH):
        wl = w_in[l]
        w_main = jnp.concatenate([wl[:, :_OFF_MI], wl[:, _OFF_GQKV:_OFF_GBETA], wl[:, _OFF_GATE:]], axis=1).astype(_BF16)
        w_small = jnp.concatenate([wl[:, _OFF_MI:_OFF_GQKV], wl[:, _OFF_GBETA:_OFF_GATE],
                                   jnp.zeros((d, N_SMALL - 2 * MLSTM_HEADS - 2 * GDN_HEADS), _F32)], axis=1).astype(_BF16)
        h = _rmsnorm(x, norm1_g[l], _BF16)
        proj_main = _mm(h, w_main, _F32, BM, 1024, "in_proj")
        proj_small = _mm(h, w_small, _F32, BM, N_SMALL, "in_proj_gates")
        mixer_w = (conv_a_w[l], ret_norm_g[l], mlstm_b_i[l], mlstm_b_f[l], mlstm_norm_g[l],
                   gdn_conv_w[l], gdn_a_log[l], gdn_dt_bias[l], gdn_norm_g[l])
        init_p = tuple(jnp.zeros((bp,) + s.shape[2:], _F32) for s in sample_state)
        ys_p, st_p = _mixers(proj_main, proj_small, 0, bp, tp, chunk_p, conv_rows_p, pos_p, init_p, *mixer_w)
        ys_s, st_s = _mixers(proj_main, proj_small, mp, bs, ts, ts, ts, pos_s,
                             tuple(s[l] for s in sample_state), *mixer_w)
        new_p.append(st_p)
        new_s.append(st_s)
        ys = tuple(jnp.concatenate([a, b], axis=0) for a, b in zip(ys_p, ys_s))
        merged = _merge(ys, proj_main, w_branch[l].astype(_BF16), BM, 512)
        x = _mm_res(merged, w_out[l].astype(_BF16), x, BM, 1024, "out_proj")
        if l % 2 == 0:
            h2 = _rmsnorm(x, norm2_g[l], _BF16)
            act = _ffn_gu(h2, ffn_w_gu[l // 2].astype(_BF16), BM, 512)
            x = _ffn_down(act, ffn_w_down[l // 2].astype(_BF16), x, BM, 1024, 2048)
        else:
            x = _moe(x, norm2_g[l], moe_w_router[l // 2], moe_w_gu[l // 2].astype(_BF16),
                     moe_w_down[l // 2].astype(_BF16))
    y = _rmsnorm(x, norm_f_g, _F32)
    outs = [y[:mp].reshape(bp, tp, d), y[mp:].reshape(bs, ts, d)]
    for i in range(7):
        outs.append(jnp.stack([new_p[l][i] for l in range(DEPTH)]))
        outs.append(jnp.stack([new_s[l][i] for l in range(DEPTH)]))
    return tuple(outs)
```

```python
import functools

import jax
import jax.numpy as jnp
from jax import lax
from jax.experimental import pallas as pl
from jax.experimental.pallas import tpu as pltpu

D_MODEL = 4096
DEPTH = 2
N_BRANCH = 4
D_MIX = D_MODEL // 4
CONV_A_WIDTH = 3
RET_HEADS = 4
RET_DK = D_MIX // RET_HEADS
MLSTM_HEADS = 4
MLSTM_DK = D_MIX // MLSTM_HEADS
GDN_HEADS = 8
GDN_DK = D_MIX // GDN_HEADS
GDN_CONV = 4
ROPE_BASE = 10000.0
N_EXPERTS = 8
TOP_K = 2
EPS = 1e-6
PAST_LEN = 2048

_OFF_MI = 11 * D_MIX
_OFF_GQKV = _OFF_MI + 2 * MLSTM_HEADS
_OFF_GBETA = _OFF_GQKV + 4 * D_MIX
_OFF_GATE = _OFF_GBETA + 2 * GDN_HEADS
N_MAIN = 15 * D_MIX + N_BRANCH * D_MODEL
N_SMALL = 128
_PS_MI, _PS_MF, _PS_BETA, _PS_A = 0, MLSTM_HEADS, 2 * MLSTM_HEADS, 2 * MLSTM_HEADS + GDN_HEADS

V7X_VMEM_LIMIT = 56 * 1024 * 1024
BM = 768
BM_MOE = 512
MIX_CHUNK = 256
CONV_ROWS = 512
GDN_HEADS_PER_STEP = 2

_BF16 = jnp.bfloat16
_F32 = jnp.float32


def _cparams(sem):
    return pltpu.CompilerParams(dimension_semantics=sem, vmem_limit_bytes=V7X_VMEM_LIMIT)


def _rmsnorm_kernel(x_ref, g_ref, o_ref):
    x = x_ref[...]
    y = x * lax.rsqrt(jnp.mean(x * x, axis=-1, keepdims=True) + EPS)
    o_ref[...] = (y * g_ref[...]).astype(o_ref.dtype)


def _rmsnorm(x, g, out_dtype, bm=256):
    m, d = x.shape
    return pl.pallas_call(
        _rmsnorm_kernel,
        grid=(m // bm,),
        in_specs=[pl.BlockSpec((bm, d), lambda i: (i, 0)), pl.BlockSpec((1, d), lambda i: (0, 0))],
        out_specs=pl.BlockSpec((bm, d), lambda i: (i, 0)),
        out_shape=jax.ShapeDtypeStruct((m, d), out_dtype),
        compiler_params=_cparams(("parallel",)),
        name="rmsnorm",
    )(x, g.reshape(1, d))


def _mm_kernel(x_ref, w_ref, o_ref):
    o_ref[...] = jnp.dot(x_ref[...], w_ref[...], preferred_element_type=_F32).astype(o_ref.dtype)


def _mm(x, w, out_dtype, bm, bn, name):
    m, k = x.shape
    n = w.shape[1]
    return pl.pallas_call(
        _mm_kernel,
        grid=(m // bm, n // bn),
        in_specs=[pl.BlockSpec((bm, k), lambda i, j: (i, 0)), pl.BlockSpec((k, bn), lambda i, j: (0, j))],
        out_specs=pl.BlockSpec((bm, bn), lambda i, j: (i, j)),
        out_shape=jax.ShapeDtypeStruct((m, n), out_dtype),
        compiler_params=_cparams(("parallel", "arbitrary")),
        name=name,
    )(x, w)


def _mm_res_kernel(x_ref, w_ref, r_ref, o_ref):
    o_ref[...] = r_ref[...] + jnp.dot(x_ref[...], w_ref[...], preferred_element_type=_F32)


def _mm_res(x, w, res, bm, bn, name):
    m, k = x.shape
    n = w.shape[1]
    return pl.pallas_call(
        _mm_res_kernel,
        grid=(m // bm, n // bn),
        in_specs=[pl.BlockSpec((bm, k), lambda i, j: (i, 0)), pl.BlockSpec((k, bn), lambda i, j: (0, j)),
                  pl.BlockSpec((bm, bn), lambda i, j: (i, j))],
        out_specs=pl.BlockSpec((bm, bn), lambda i, j: (i, j)),
        out_shape=jax.ShapeDtypeStruct((m, n), _F32),
        compiler_params=_cparams(("parallel", "arbitrary")),
        name=name,
    )(x, w, res)


def _merge_kernel(ya_ref, yr_ref, ym_ref, yg_ref, g0_ref, g1_ref, g2_ref, g3_ref, w_ref, o_ref):
    acc = None
    for b, (y_ref, g_ref) in enumerate(((ya_ref, g0_ref), (yr_ref, g1_ref), (ym_ref, g2_ref), (yg_ref, g3_ref))):
        t = jax.nn.sigmoid(g_ref[...]) * jnp.dot(y_ref[...], w_ref[b], preferred_element_type=_F32)
        acc = t if acc is None else acc + t
    o_ref[...] = acc.astype(o_ref.dtype)


def _merge(ys, proj_main, w_branch, bm, bn):
    m, dm = ys[0].shape
    n = w_branch.shape[2]
    gate_blk0 = (15 * D_MIX) // bn
    per_branch = n // bn

    def gate_spec(b):
        return pl.BlockSpec((bm, bn), lambda i, j: (i, gate_blk0 + b * per_branch + j))

    y_spec = pl.BlockSpec((bm, dm), lambda i, j: (i, 0))
    return pl.pallas_call(
        _merge_kernel,
        grid=(m // bm, n // bn),
        in_specs=[y_spec, y_spec, y_spec, y_spec, gate_spec(0), gate_spec(1), gate_spec(2), gate_spec(3),
                  pl.BlockSpec((N_BRANCH, dm, bn), lambda i, j: (0, 0, j))],
        out_specs=pl.BlockSpec((bm, bn), lambda i, j: (i, j)),
        out_shape=jax.ShapeDtypeStruct((m, n), _BF16),
        compiler_params=_cparams(("parallel", "arbitrary")),
        name="branch_merge",
    )(*ys, proj_main, proj_main, proj_main, proj_main, w_branch)


def _gu_kernel(x_ref, wg_ref, wu_ref, o_ref):
    x = x_ref[...]
    g = jnp.dot(x, wg_ref[...], preferred_element_type=_F32)
    u = jnp.dot(x, wu_ref[...], preferred_element_type=_F32)
    o_ref[...] = (g * jax.nn.sigmoid(g) * u).astype(o_ref.dtype)


def _ffn_gu(x, w_gu, bm, bn):
    m, k = x.shape
    f = w_gu.shape[1] // 2
    nb = f // bn
    return pl.pallas_call(
        _gu_kernel,
        grid=(m // bm, nb),
        in_specs=[pl.BlockSpec((bm, k), lambda i, j: (i, 0)),
                  pl.BlockSpec((k, bn), lambda i, j: (0, j)),
                  pl.BlockSpec((k, bn), lambda i, j: (0, nb + j))],
        out_specs=pl.BlockSpec((bm, bn), lambda i, j: (i, j)),
        out_shape=jax.ShapeDtypeStruct((m, f), _BF16),
        compiler_params=_cparams(("parallel", "arbitrary")),
        name="ffn_gate_up",
    )(x, w_gu, w_gu)


def _down_kernel(x_ref, w_ref, r_ref, o_ref, acc_ref):
    k = pl.program_id(2)

    @pl.when(k == 0)
    def _():
        acc_ref[...] = r_ref[...]

    acc_ref[...] += jnp.dot(x_ref[...], w_ref[...], preferred_element_type=_F32)

    @pl.when(k == pl.num_programs(2) - 1)
    def _():
        o_ref[...] = acc_ref[...]


def _ffn_down(act, w_down, res, bm, bn, bk):
    m, f = act.shape
    n = w_down.shape[1]
    return pl.pallas_call(
        _down_kernel,
        grid=(m // bm, n // bn, f // bk),
        in_specs=[pl.BlockSpec((bm, bk), lambda i, j, k: (i, k)),
                  pl.BlockSpec((bk, bn), lambda i, j, k: (k, j)),
                  pl.BlockSpec((bm, bn), lambda i, j, k: (i, j))],
        out_specs=pl.BlockSpec((bm, bn), lambda i, j, k: (i, j)),
        out_shape=jax.ShapeDtypeStruct((m, n), _F32),
        scratch_shapes=[pltpu.VMEM((bm, bn), _F32)],
        compiler_params=_cparams(("parallel", "arbitrary", "arbitrary")),
        name="ffn_down",
    )(act, w_down, res)


def _router_kernel(x_ref, g_ref, w_ref, h_ref, l_ref):
    x = x_ref[...]
    h = x * lax.rsqrt(jnp.mean(x * x, axis=-1, keepdims=True) + EPS) * g_ref[...]
    h_ref[...] = h.astype(h_ref.dtype)
    logits = jnp.dot(h, w_ref[...], preferred_element_type=_F32, precision=lax.Precision.HIGHEST)
    lane = lax.broadcasted_iota(jnp.int32, logits.shape, 1)
    lg = jnp.where(lane < N_EXPERTS, logits, -jnp.inf)
    v1 = jnp.max(lg, axis=-1, keepdims=True)
    i1 = jnp.min(jnp.where(lg == v1, lane, N_SMALL), axis=-1, keepdims=True)
    lg2 = jnp.where(lane == i1, -jnp.inf, lg)
    v2 = jnp.max(lg2, axis=-1, keepdims=True)
    i2 = jnp.min(jnp.where(lg2 == v2, lane, N_SMALL), axis=-1, keepdims=True)
    e2 = jnp.exp(v2 - v1)
    den = 1.0 + e2
    l_ref[...] = jnp.where(lane == 0, 1.0 / den,
                           jnp.where(lane == 1, e2 / den,
                                     jnp.where(lane == 2, i1.astype(_F32),
                                               jnp.where(lane == 3, i2.astype(_F32), 0.0))))


def _norm_router(x, g, w_router_pad, bm=256):
    m, d = x.shape
    ne = w_router_pad.shape[1]
    return pl.pallas_call(
        _router_kernel,
        grid=(m // bm,),
        in_specs=[pl.BlockSpec((bm, d), lambda i: (i, 0)), pl.BlockSpec((1, d), lambda i: (0, 0)),
                  pl.BlockSpec((d, ne), lambda i: (0, 0))],
        out_specs=[pl.BlockSpec((bm, d), lambda i: (i, 0)), pl.BlockSpec((bm, ne), lambda i: (i, 0))],
        out_shape=[jax.ShapeDtypeStruct((m, d), _BF16), jax.ShapeDtypeStruct((m, ne), _F32)],
        compiler_params=_cparams(("parallel",)),
        name="norm_router",
    )(x, g.reshape(1, d), w_router_pad)


def _moe_gu_kernel(te_ref, nt_ref, x_ref, wg_ref, wu_ref, o_ref):
    @pl.when(pl.program_id(0) < nt_ref[0])
    def _():
        x = x_ref[...]
        g = jnp.dot(x, wg_ref[...], preferred_element_type=_F32)
        u = jnp.dot(x, wu_ref[...], preferred_element_type=_F32)
        o_ref[...] = (g * jax.nn.sigmoid(g) * u).astype(o_ref.dtype)

    @pl.when(pl.program_id(0) >= nt_ref[0])
    def _():
        o_ref[...] = jnp.zeros_like(o_ref)


def _moe_gu(tile_expert, n_tiles, xs, w_gu, bm, bn):
    m, k = xs.shape
    f = w_gu.shape[2] // 2
    nb = f // bn
    grid_spec = pltpu.PrefetchScalarGridSpec(
        num_scalar_prefetch=2,
        grid=(m // bm, nb),
        in_specs=[pl.BlockSpec((bm, k), lambda t, j, te, nt: (t, 0)),
                  pl.BlockSpec((None, k, bn), lambda t, j, te, nt: (te[t], 0, j)),
                  pl.BlockSpec((None, k, bn), lambda t, j, te, nt: (te[t], 0, nb + j))],
        out_specs=pl.BlockSpec((bm, bn), lambda t, j, te, nt: (t, j)),
    )
    return pl.pallas_call(
        _moe_gu_kernel,
        grid_spec=grid_spec,
        out_shape=jax.ShapeDtypeStruct((m, f), _BF16),
        compiler_params=_cparams(("parallel", "arbitrary")),
        name="moe_gate_up",
    )(tile_expert, n_tiles, xs, w_gu, w_gu)


def _moe_down_kernel(te_ref, nt_ref, x_ref, w_ref, o_ref, acc_ref):
    k = pl.program_id(2)

    @pl.when(k == 0)
    def _():
        acc_ref[...] = jnp.zeros_like(acc_ref)

    @pl.when(pl.program_id(0) < nt_ref[0])
    def _():
        acc_ref[...] += jnp.dot(x_ref[...], w_ref[...], preferred_element_type=_F32)

    @pl.when(k == pl.num_programs(2) - 1)
    def _():
        o_ref[...] = acc_ref[...]


def _moe_down(tile_expert, n_tiles, act, w_down, bm, bn, bk):
    m, f = act.shape
    n = w_down.shape[2]
    grid_spec = pltpu.PrefetchScalarGridSpec(
        num_scalar_prefetch=2,
        grid=(m // bm, n // bn, f // bk),
        in_specs=[pl.BlockSpec((bm, bk), lambda t, j, k, te, nt: (t, k)),
                  pl.BlockSpec((None, bk, bn), lambda t, j, k, te, nt: (te[t], k, j))],
        out_specs=pl.BlockSpec((bm, bn), lambda t, j, k, te, nt: (t, j)),
        scratch_shapes=[pltpu.VMEM((bm, bn), _F32)],
    )
    return pl.pallas_call(
        _moe_down_kernel,
        grid_spec=grid_spec,
        out_shape=jax.ShapeDtypeStruct((m, n), _F32),
        compiler_params=_cparams(("parallel", "arbitrary", "arbitrary")),
        name="moe_down",
    )(tile_expert, n_tiles, act, w_down)


def _moe(x, norm_g, w_router, w_gu, w_down):
    m, d = x.shape
    bm = BM_MOE
    wr = jnp.zeros((d, N_SMALL), _F32).at[:, :N_EXPERTS].set(w_router)
    h, route = _norm_router(x, norm_g, wr)
    weights = route[:, 0:TOP_K]
    top_i = route[:, TOP_K:2 * TOP_K].astype(jnp.int32)
    n_pairs = m * TOP_K
    e_flat = top_i.reshape(n_pairs)
    onehot = (e_flat[:, None] == jnp.arange(N_EXPERTS)[None, :]).astype(jnp.int32)
    blk = N_SMALL
    tri = (jnp.arange(blk)[:, None] >= jnp.arange(blk)[None, :]).astype(_F32)
    within = jnp.einsum('ij,bjk->bik', tri, onehot.astype(_F32).reshape(n_pairs // blk, blk, N_EXPERTS),
                        precision=lax.Precision.HIGHEST)
    tot = within[:, -1, :]
    csum = (within + (jnp.cumsum(tot, axis=0) - tot)[:, None, :]).reshape(n_pairs, N_EXPERTS).astype(jnp.int32)
    counts = csum[-1]
    rank = jnp.sum(csum * onehot, axis=1) - 1
    padded = ((counts + bm - 1) // bm) * bm
    ends = jnp.cumsum(padded)
    start = ends - padded
    dest = start[e_flat] + rank
    n_tiles_max = n_pairs // bm + N_EXPERTS
    n_slots = n_tiles_max * bm
    n_tiles = (ends[-1] // bm).astype(jnp.int32)
    src_tok = jnp.zeros((n_slots,), jnp.int32).at[dest].set(jnp.arange(n_pairs, dtype=jnp.int32) // TOP_K)
    tile_start = jnp.arange(n_tiles_max, dtype=jnp.int32) * bm
    tile_expert = jnp.sum((tile_start[:, None] >= ends[None, :]).astype(jnp.int32), axis=1)
    last_e = jnp.max(jnp.where(counts > 0, jnp.arange(N_EXPERTS), 0)).astype(jnp.int32)
    tile_expert = jnp.minimum(tile_expert, last_e).astype(jnp.int32)
    xs = jnp.take(h, src_tok, axis=0)
    act = _moe_gu(tile_expert, n_tiles.reshape(1), xs, w_gu, bm, 1024)
    ys = _moe_down(tile_expert, n_tiles.reshape(1), act, w_down, bm, 1024, 3584)
    dest2 = dest.reshape(m, TOP_K)
    return (x + weights[:, 0:1] * jnp.take(ys, dest2[:, 0], axis=0)
            + weights[:, 1:2] * jnp.take(ys, dest2[:, 1], axis=0))


def _dot(a, b):
    return jnp.dot(a.astype(_BF16), b.astype(_BF16), preferred_element_type=_F32)


def _dot_nt(a, b):
    return lax.dot_general(a.astype(_BF16), b.astype(_BF16), (((1,), (1,)), ((), ())), preferred_element_type=_F32)


def _dot_tn(a, b):
    return lax.dot_general(a.astype(_BF16), b.astype(_BF16), (((0,), (0,)), ((), ())), preferred_element_type=_F32)


def _split2(x):
    hi = x.astype(_BF16)
    return hi, (x - hi.astype(_F32)).astype(_BF16)


def _dot_f32(a, b):
    ah, al = _split2(a)
    bh, bl = _split2(b)
    return (jnp.dot(ah, bh, preferred_element_type=_F32)
            + (jnp.dot(ah, bl, preferred_element_type=_F32) + jnp.dot(al, bh, preferred_element_type=_F32)))


def _dot_cum(tril, x):
    t = tril.astype(_BF16)
    hi = x.astype(_BF16)
    r = x - hi.astype(_F32)
    mid = r.astype(_BF16)
    lo = (r - mid.astype(_F32)).astype(_BF16)
    return (jnp.dot(t, hi, preferred_element_type=_F32)
            + (jnp.dot(t, mid, preferred_element_type=_F32) + jnp.dot(t, lo, preferred_element_type=_F32)))


def _col(x, idx):
    lane = lax.broadcasted_iota(jnp.int32, x.shape, 1)
    return jnp.sum(jnp.where(lane == idx, x, 0.0), axis=-1, keepdims=True)


def _iotas(n):
    return lax.broadcasted_iota(jnp.int32, (n, n), 0), lax.broadcasted_iota(jnp.int32, (n, n), 1)


def _softplus(x):
    return jnp.maximum(x, 0.0) + jnp.log1p(jnp.exp(-jnp.abs(x)))


def _shifted_rows(u, k, carry):
    r = pltpu.roll(u, k, axis=0)
    row = lax.broadcasted_iota(jnp.int32, u.shape, 0)
    nc = carry.shape[0]
    for i in range(k):
        r = jnp.where(row == i, carry[nc - k + i:nc - k + i + 1, :], r)
    return r


def _causal_conv_rows(u, w, carry):
    kw = w.shape[0]
    acc = w[kw - 1:kw, :] * u
    for k in range(1, kw):
        acc = acc + w[kw - 1 - k:kw - k, :] * _shifted_rows(u, k, carry)
    return acc


def _conv_a_kernel(b_ref, c_ref, h_ref, w_ref, hist_ref, y_ref, st_ref, carry_ref):
    @pl.when(pl.program_id(2) == 0)
    def _():
        carry_ref[...] = hist_ref[...]

    u = c_ref[...] * h_ref[...]
    acc = _causal_conv_rows(u, w_ref[...], carry_ref[...])
    y_ref[...] = (b_ref[...] * acc).astype(y_ref.dtype)
    tail = u[u.shape[0] - (CONV_A_WIDTH - 1):, :]
    carry_ref[...] = tail
    st_ref[...] = tail


def _conv_g_kernel(x_ref, w_ref, hist_ref, y_ref, st_ref, carry_ref):
    @pl.when(pl.program_id(2) == 0)
    def _():
        carry_ref[...] = hist_ref[...]

    u = x_ref[...]
    acc = _causal_conv_rows(u, w_ref[...], carry_ref[...])
    y_ref[...] = acc * jax.nn.sigmoid(acc)
    tail = u[u.shape[0] - (GDN_CONV - 1):, :]
    carry_ref[...] = tail
    st_ref[...] = tail


def _conv_calls(proj_main, row0, nb, t, tb, conv_a_w, hist_a, gdn_conv_w, hist_g):
    nt = t // tb
    r0 = row0 // tb
    cb = D_MIX

    def rows(c0):
        return pl.BlockSpec((tb, cb), lambda b, j, i: (r0 + b * nt + i, c0 + j))

    ka, kg = CONV_A_WIDTH, GDN_CONV
    y_a, st_a = pl.pallas_call(
        _conv_a_kernel,
        grid=(nb, 1, nt),
        in_specs=[rows(0), rows(1), rows(2),
                  pl.BlockSpec((ka, cb), lambda b, j, i: (0, j)),
                  pl.BlockSpec((None, ka - 1, cb), lambda b, j, i: (b, 0, j))],
        out_specs=[pl.BlockSpec((tb, cb), lambda b, j, i: (b * nt + i, j)),
                   pl.BlockSpec((None, ka - 1, cb), lambda b, j, i: (b, 0, j))],
        out_shape=[jax.ShapeDtypeStruct((nb * t, D_MIX), _BF16),
                   jax.ShapeDtypeStruct((nb, ka - 1, D_MIX), _F32)],
        scratch_shapes=[pltpu.VMEM((ka - 1, cb), _F32)],
        compiler_params=_cparams(("parallel", "parallel", "arbitrary")),
        name="conv_a",
    )(proj_main, proj_main, proj_main, conv_a_w, hist_a)
    gqkv, st_g = pl.pallas_call(
        _conv_g_kernel,
        grid=(nb, 3, nt),
        in_specs=[rows(11),
                  pl.BlockSpec((kg, cb), lambda b, j, i: (0, j)),
                  pl.BlockSpec((None, kg - 1, cb), lambda b, j, i: (b, 0, j))],
        out_specs=[pl.BlockSpec((tb, cb), lambda b, j, i: (b * nt + i, j)),
                   pl.BlockSpec((None, kg - 1, cb), lambda b, j, i: (b, 0, j))],
        out_shape=[jax.ShapeDtypeStruct((nb * t, 3 * D_MIX), _F32),
                   jax.ShapeDtypeStruct((nb, kg - 1, 3 * D_MIX), _F32)],
        scratch_shapes=[pltpu.VMEM((kg - 1, cb), _F32)],
        compiler_params=_cparams(("parallel", "parallel", "arbitrary")),
        name="conv_gdn",
    )(proj_main, gdn_conv_w, hist_g)
    return y_a, st_a, gqkv, st_g


def _head_norm_gate(o, gain, gate):
    oc = o - jnp.mean(o, axis=-1, keepdims=True)
    y = oc * lax.rsqrt(jnp.mean(oc * oc, axis=-1, keepdims=True) + EPS)
    return gate * (y * gain)


def _ret_kernel(q_ref, k_ref, v_ref, g_ref, cos_ref, sin_ref, ng_ref, lg_ref, s0_ref, y_ref, s_ref):
    L = q_ref.shape[0]

    @pl.when(pl.program_id(2) == 0)
    def _():
        s_ref[...] = s0_ref[...]

    lg = lg_ref[...][:, :1]
    cos, sin = cos_ref[...], sin_ref[...]
    half = RET_DK // 2

    def rope(x):
        x1, x2 = x[:, :half], x[:, half:]
        return jnp.concatenate([x1 * cos - x2 * sin, x1 * sin + x2 * cos], axis=-1)

    rq = rope(q_ref[...])
    rk = rope(k_ref[...]) * RET_DK ** -0.5
    v = v_ref[...]
    row, col = _iotas(L)
    intra = jnp.where(row >= col, jnp.exp((row - col).astype(_F32) * lg), 0.0)
    ridx = lax.broadcasted_iota(jnp.int32, (L, 1), 0).astype(_F32)
    inter = jnp.exp((ridx + 1.0) * lg)
    w_k = jnp.exp((L - 1.0 - ridx) * lg)
    dec = jnp.exp(L * lg)
    S = s_ref[...]
    s = _dot_nt(rq, rk) * intra
    o = _dot(s, v) + inter * _dot(rq, S)
    s_ref[...] = dec * S + _dot_tn(rk * w_k, v)
    g = g_ref[...]
    y_ref[...] = _head_norm_gate(o, ng_ref[...], g * jax.nn.sigmoid(g)).astype(y_ref.dtype)


def _mlstm_kernel(q_ref, k_ref, v_ref, og_ref, ps_ref, bias_ref, ng_ref, c0_ref, n0_ref, m0_ref,
                  y_ref, c_ref, n_ref, m_ref):
    L = q_ref.shape[0]
    h = pl.program_id(1)

    @pl.when(pl.program_id(2) == 0)
    def _():
        c_ref[...] = c0_ref[...]
        n_ref[...] = n0_ref[...]
        m_ref[...] = m0_ref[...]

    z = ps_ref[...] + bias_ref[...]
    li = _col(z, _PS_MI + h)
    fp = _col(z, _PS_MF + h)
    lf = jnp.minimum(fp, 0.0) - jnp.log1p(jnp.exp(-jnp.abs(fp)))
    row, col = _iotas(L)
    causal = row >= col
    tril = causal.astype(_F32)
    logd = _dot_cum(tril, jnp.where(row > col, lf, jnp.where(row == col, li, 0.0)))
    b = (logd[:, :1] - li[0:1, :]) + lf[0:1, :]
    m_prev = m_ref[...][:, :1]
    inter = b + m_prev
    mt = jnp.maximum(inter, jnp.max(jnp.where(causal, logd, -jnp.inf), axis=-1, keepdims=True))
    dmat = jnp.where(causal, jnp.exp(logd - mt), 0.0)
    q = q_ref[...]
    k = k_ref[...] * MLSTM_DK ** -0.5
    v = v_ref[...]
    s = _dot_nt(q, k) * dmat
    g_inter = jnp.exp(inter - mt)
    C = c_ref[...]
    n = n_ref[...]
    num = _dot(s, v) + g_inter * _dot(q, C)
    den = jnp.sum(s, axis=-1, keepdims=True) + g_inter * jnp.sum(q * n, axis=-1, keepdims=True)
    hh = num / jnp.maximum(jnp.abs(den), 1.0)
    m_last = mt[L - 1:L, :]
    b_last = b[L - 1:L, :]
    kw = k * jnp.exp(b_last - b + li - m_last)
    d_c = jnp.exp(b_last + m_prev - m_last)
    c_ref[...] = d_c * C + _dot_tn(kw, v)
    n_ref[...] = d_c * n + jnp.sum(kw, axis=0, keepdims=True)
    m_ref[...] = jnp.broadcast_to(m_last, m_ref.shape)
    y_ref[...] = _head_norm_gate(hh, ng_ref[...], jax.nn.sigmoid(og_ref[...])).astype(y_ref.dtype)


def _unit_lower_inverse(a, n):
    row, col = _iotas(n)
    eye = (row == col).astype(_F32)

    def same_block(shift):
        return jnp.right_shift(row, shift) == jnp.right_shift(col, shift)

    ad = jnp.where(same_block(3), a, 0.0)
    a2 = _dot_f32(ad, ad)
    a4 = _dot_f32(a2, a2)
    t = _dot_f32(_dot_f32(eye - ad, eye + a2), eye + a4)
    shift = 3
    while (1 << shift) < n:
        off = jnp.logical_and(same_block(shift + 1), jnp.logical_not(same_block(shift)))
        t = t - _dot_f32(_dot_f32(t, jnp.where(off, a, 0.0)), t)
        shift += 1
    return t


def _gdn_kernel(q_ref, k_ref, v_ref, z_ref, ps_ref, alog_ref, dtb_ref, ng_ref, s0_ref, y_ref, s_ref):
    L = q_ref.shape[0]

    @pl.when(pl.program_id(2) == 0)
    def _():
        s_ref[...] = s0_ref[...]

    ps = ps_ref[...]
    ps_a = ps + dtb_ref[...]
    row, col = _iotas(L)
    incl = row >= col
    strict = row > col
    tril = incl.astype(_F32)

    def l2n(x):
        return x * lax.rsqrt(jnp.sum(x * x, axis=-1, keepdims=True) + EPS)

    for j in range(GDN_HEADS_PER_STEP):
        h = pl.program_id(1) * GDN_HEADS_PER_STEP + j
        lanes = slice(j * GDN_DK, (j + 1) * GDN_DK)
        beta = jax.nn.sigmoid(_col(ps, _PS_BETA + h))
        neg_a = -jnp.exp(_col(alog_ref[...], _PS_A + h))
        g = neg_a * _softplus(_col(ps_a, _PS_A + h))
        q = l2n(q_ref[:, lanes]) * GDN_DK ** -0.5
        k = l2n(k_ref[:, lanes])
        v = v_ref[:, lanes]
        gd = _dot_cum(tril, jnp.where(strict, g, 0.0))
        G = gd[:, :1] + g[0:1, :]
        decay = jnp.where(incl, jnp.exp(gd), 0.0)
        a = jnp.where(strict, beta * _dot_nt(k, k) * decay, 0.0)
        e_g = jnp.exp(G)
        rhs = jnp.concatenate([v * beta, k * (beta * e_g)], axis=-1)
        sol = _dot_f32(_unit_lower_inverse(a, L), rhs)
        u, w = sol[:, :GDN_DK], sol[:, GDN_DK:]
        S = s_ref[j]
        delta = u - _dot(w, S)
        o = _dot(_dot_nt(q, k) * decay, delta) + e_g * _dot(q, S)
        g_last = G[L - 1:L, :]
        s_ref[j] = jnp.exp(g_last) * S + _dot_tn(k * jnp.exp(g_last - G), delta)
        go = o * lax.rsqrt(jnp.mean(o * o, axis=-1, keepdims=True) + EPS) * ng_ref[...]
        z = z_ref[:, lanes]
        y_ref[:, lanes] = (go * (z * jax.nn.sigmoid(z))).astype(y_ref.dtype)


def _rope_tables(pos):
    half = RET_DK // 2
    inv = ROPE_BASE ** (-jnp.arange(half, dtype=_F32) / half)
    ang = pos.astype(_F32)[:, None] * inv[None, :]
    return jnp.cos(ang), jnp.sin(ang)


def _lane_row(vals, off):
    return jnp.zeros((1, N_SMALL), _F32).at[0, off:off + vals.shape[0]].set(vals)


def _mixers(proj_main, proj_small, row0, nb, t, L, tb, pos, st, conv_a_w, ret_norm_g, mlstm_b_i, mlstm_b_f,
            mlstm_norm_g, gdn_conv_w, gdn_a_log, gdn_dt_bias, gdn_norm_g):
    conv_a_buf, ret_S, m_C, m_n, m_m, g_buf, g_S = st
    nc = t // L
    r0 = row0 // L
    sem = ("parallel", "parallel", "arbitrary")
    y_a, st_a, gqkv, st_g = _conv_calls(proj_main, row0, nb, t, tb, conv_a_w, conv_a_buf, gdn_conv_w, g_buf)

    def main_cols(group, width, heads):
        return pl.BlockSpec((L, width), lambda b, h, c: (r0 + b * nc + c, group * heads + h))

    def out_cols(width):
        return pl.BlockSpec((L, width), lambda b, h, c: (b * nc + c, h))

    def state_spec(*dims):
        return pl.BlockSpec((None, None) + dims, lambda b, h, c: (b, h) + (0,) * len(dims))

    def gain_spec(width):
        return pl.BlockSpec((1, width), lambda b, h, c: (0, h))

    small_spec = pl.BlockSpec((L, N_SMALL), lambda b, h, c: (r0 + b * nc + c, 0))
    lane_spec = pl.BlockSpec((1, N_SMALL), lambda b, h, c: (0, 0))

    cos, sin = _rope_tables(pos)
    log_gamma = jnp.log1p(-jnp.exp2(-5.0 - jnp.arange(RET_HEADS, dtype=_F32)))
    lg_tab = jnp.broadcast_to(log_gamma[:, None, None], (RET_HEADS, 1, N_SMALL))
    rope_spec = pl.BlockSpec((L, RET_DK // 2), lambda b, h, c: (c, 0))
    y_r, ret_new = pl.pallas_call(
        _ret_kernel,
        grid=(nb, RET_HEADS, nc),
        in_specs=[main_cols(3, RET_DK, RET_HEADS), main_cols(4, RET_DK, RET_HEADS),
                  main_cols(5, RET_DK, RET_HEADS), main_cols(6, RET_DK, RET_HEADS),
                  rope_spec, rope_spec, gain_spec(RET_DK),
                  pl.BlockSpec((None, 1, N_SMALL), lambda b, h, c: (h, 0, 0)),
                  state_spec(RET_DK, RET_DK)],
        out_specs=[out_cols(RET_DK), state_spec(RET_DK, RET_DK)],
        out_shape=[jax.ShapeDtypeStruct((nb * t, D_MIX), _BF16),
                   jax.ShapeDtypeStruct((nb, RET_HEADS, RET_DK, RET_DK), _F32)],
        compiler_params=_cparams(sem),
        name="retention",
    )(proj_main, proj_main, proj_main, proj_main, cos, sin, ret_norm_g.reshape(1, D_MIX), lg_tab, ret_S)

    bias_row = _lane_row(jnp.concatenate([mlstm_b_i, mlstm_b_f]), _PS_MI)
    y_m, c_new, n_new, m_new = pl.pallas_call(
        _mlstm_kernel,
        grid=(nb, MLSTM_HEADS, nc),
        in_specs=[main_cols(7, MLSTM_DK, MLSTM_HEADS), main_cols(8, MLSTM_DK, MLSTM_HEADS),
                  main_cols(9, MLSTM_DK, MLSTM_HEADS), main_cols(10, MLSTM_DK, MLSTM_HEADS),
                  small_spec, lane_spec, gain_spec(MLSTM_DK),
                  state_spec(MLSTM_DK, MLSTM_DK), state_spec(1, MLSTM_DK), state_spec(1, N_SMALL)],
        out_specs=[out_cols(MLSTM_DK), state_spec(MLSTM_DK, MLSTM_DK), state_spec(1, MLSTM_DK),
                   state_spec(1, N_SMALL)],
        out_shape=[jax.ShapeDtypeStruct((nb * t, D_MIX), _BF16),
                   jax.ShapeDtypeStruct((nb, MLSTM_HEADS, MLSTM_DK, MLSTM_DK), _F32),
                   jax.ShapeDtypeStruct((nb, MLSTM_HEADS, 1, MLSTM_DK), _F32),
                   jax.ShapeDtypeStruct((nb, MLSTM_HEADS, 1, N_SMALL), _F32)],
        compiler_params=_cparams(sem),
        name="mlstm",
    )(proj_main, proj_main, proj_main, proj_main, proj_small, bias_row, mlstm_norm_g.reshape(1, D_MIX),
      m_C, m_n.reshape(nb, MLSTM_HEADS, 1, MLSTM_DK),
      jnp.broadcast_to(m_m[:, :, None, None], (nb, MLSTM_HEADS, 1, N_SMALL)))

    hps = GDN_HEADS_PER_STEP
    n_hg = GDN_HEADS // hps

    def conv_cols(group):
        return pl.BlockSpec((L, hps * GDN_DK), lambda b, h, c: (b * nc + c, group * n_hg + h))

    gdn_state = pl.BlockSpec((None, hps, GDN_DK, GDN_DK), lambda b, h, c: (b, h, 0, 0))
    y_g, gdn_new = pl.pallas_call(
        _gdn_kernel,
        grid=(nb, n_hg, nc),
        in_specs=[conv_cols(0), conv_cols(1), conv_cols(2), main_cols(14, hps * GDN_DK, n_hg),
                  small_spec, lane_spec, lane_spec,
                  pl.BlockSpec((1, GDN_DK), lambda b, h, c: (0, 0)),
                  gdn_state],
        out_specs=[out_cols(hps * GDN_DK), gdn_state],
        out_shape=[jax.ShapeDtypeStruct((nb * t, D_MIX), _BF16),
                   jax.ShapeDtypeStruct((nb, GDN_HEADS, GDN_DK, GDN_DK), _F32)],
        compiler_params=_cparams(sem),
        name="gated_delta",
    )(gqkv, gqkv, gqkv, proj_main, proj_small, _lane_row(gdn_a_log, _PS_A), _lane_row(gdn_dt_bias, _PS_A),
      gdn_norm_g.reshape(1, GDN_DK), g_S)

    new_state = (st_a, ret_new, c_new, n_new.reshape(nb, MLSTM_HEADS, MLSTM_DK), m_new[:, :, 0, 0], st_g, gdn_new)
    return (y_a, y_r, y_m, y_g), new_state


def kernel(x_prompt, x_sample, state_conv_a, state_ret, state_mlstm_c, state_mlstm_n, state_mlstm_m, state_gdn_conv, state_gdn, norm1_g, norm2_g, norm_f_g, w_in, conv_a_w, ret_norm_g, mlstm_b_i, mlstm_b_f, mlstm_norm_g, gdn_conv_w, gdn_a_log, gdn_dt_bias, gdn_norm_g, w_branch, w_out, ffn_w_gu, ffn_w_down, moe_w_router, moe_w_gu, moe_w_down):
    bp, tp, d = x_prompt.shape
    bs, ts, _ = x_sample.shape
    mp, ms = bp * tp, bs * ts
    m = mp + ms
    x = jnp.concatenate([x_prompt.reshape(mp, d), x_sample.reshape(ms, d)], axis=0)
    pos_p = jnp.arange(tp)
    pos_s = PAST_LEN + jnp.arange(ts)
    sample_state = (state_conv_a, state_ret, state_mlstm_c, state_mlstm_n, state_mlstm_m, state_gdn_conv, state_gdn)
    chunk_p = MIX_CHUNK if tp % MIX_CHUNK == 0 else tp
    conv_rows_p = CONV_ROWS if tp % CONV_ROWS == 0 else tp
    new_p, new_s = [], []
    for l in range(DEPTH):
        wl = w_in[l]
        w_main = jnp.concatenate([wl[:, :_OFF_MI], wl[:, _OFF_GQKV:_OFF_GBETA], wl[:, _OFF_GATE:]], axis=1).astype(_BF16)
        w_small = jnp.concatenate([wl[:, _OFF_MI:_OFF_GQKV], wl[:, _OFF_GBETA:_OFF_GATE],
                                   jnp.zeros((d, N_SMALL - 2 * MLSTM_HEADS - 2 * GDN_HEADS), _F32)], axis=1).astype(_BF16)
        h = _rmsnorm(x, norm1_g[l], _BF16)
        proj_main = _mm(h, w_main, _F32, BM, 1024, "in_proj")
        proj_small = _mm(h, w_small, _F32, BM, N_SMALL, "in_proj_gates")
        mixer_w = (conv_a_w[l], ret_norm_g[l], mlstm_b_i[l], mlstm_b_f[l], mlstm_norm_g[l],
                   gdn_conv_w[l], gdn_a_log[l], gdn_dt_bias[l], gdn_norm_g[l])
        init_p = tuple(jnp.zeros((bp,) + s.shape[2:], _F32) for s in sample_state)
        ys_p, st_p = _mixers(proj_main, proj_small, 0, bp, tp, chunk_p, conv_rows_p, pos_p, init_p, *mixer_w)
        ys_s, st_s = _mixers(proj_main, proj_small, mp, bs, ts, ts, ts, pos_s,
                             tuple(s[l] for s in sample_state), *mixer_w)
        new_p.append(st_p)
        new_s.append(st_s)
        ys = tuple(jnp.concatenate([a, b], axis=0) for a, b in zip(ys_p, ys_s))
        merged = _merge(ys, proj_main, w_branch[l].astype(_BF16), BM, 512)
        x = _mm_res(merged, w_out[l].astype(_BF16), x, BM, 1024, "out_proj")
        if l % 2 == 0:
            h2 = _rmsnorm(x, norm2_g[l], _BF16)
            act = _ffn_gu(h2, ffn_w_gu[l // 2].astype(_BF16), BM, 512)
            x = _ffn_down(act, ffn_w_down[l // 2].astype(_BF16), x, BM, 1024, 3584)
        else:
            x = _moe(x, norm2_g[l], moe_w_router[l // 2], moe_w_gu[l // 2].astype(_BF16),
                     moe_w_down[l // 2].astype(_BF16))
    y = _rmsnorm(x, norm_f_g, _F32)
    outs = [y[:mp].reshape(bp, tp, d), y[mp:].reshape(bs, ts, d)]
    for i in range(7):
        outs.append(jnp.stack([new_p[l][i] for l in range(DEPTH)]))
        outs.append(jnp.stack([new_s[l][i] for l in range(DEPTH)]))
    return tuple(outs)
```

```python
import functools

import jax
import jax.numpy as jnp
from jax import lax
from jax.experimental import pallas as pl
from jax.experimental.pallas import tpu as pltpu

D_MODEL = 4096
DEPTH = 2
N_BRANCH = 4
D_MIX = D_MODEL // 4
CONV_A_WIDTH = 3
RET_HEADS = 4
RET_DK = D_MIX // RET_HEADS
MLSTM_HEADS = 4
MLSTM_DK = D_MIX // MLSTM_HEADS
GDN_HEADS = 8
GDN_DK = D_MIX // GDN_HEADS
GDN_CONV = 4
ROPE_BASE = 10000.0
N_EXPERTS = 8
TOP_K = 2
EPS = 1e-6
PAST_LEN = 2048

_OFF_MI = 11 * D_MIX
_OFF_GQKV = _OFF_MI + 2 * MLSTM_HEADS
_OFF_GBETA = _OFF_GQKV + 4 * D_MIX
_OFF_GATE = _OFF_GBETA + 2 * GDN_HEADS
N_MAIN = 15 * D_MIX + N_BRANCH * D_MODEL
N_SMALL = 128
_PS_MI, _PS_MF, _PS_BETA, _PS_A = 0, MLSTM_HEADS, 2 * MLSTM_HEADS, 2 * MLSTM_HEADS + GDN_HEADS

V7X_VMEM_LIMIT = 56 * 1024 * 1024
BM = 768
BM_MOE = 512
MIX_CHUNK = 256
CONV_ROWS = 512
GDN_HEADS_PER_STEP = 2

_BF16 = jnp.bfloat16
_F32 = jnp.float32


def _cparams(sem):
    return pltpu.CompilerParams(dimension_semantics=sem, vmem_limit_bytes=V7X_VMEM_LIMIT)


def _rmsnorm_kernel(x_ref, g_ref, o_ref):
    x = x_ref[...]
    y = x * lax.rsqrt(jnp.mean(x * x, axis=-1, keepdims=True) + EPS)
    o_ref[...] = (y * g_ref[...]).astype(o_ref.dtype)


def _rmsnorm(x, g, out_dtype, bm=256):
    m, d = x.shape
    return pl.pallas_call(
        _rmsnorm_kernel,
        grid=(m // bm,),
        in_specs=[pl.BlockSpec((bm, d), lambda i: (i, 0)), pl.BlockSpec((1, d), lambda i: (0, 0))],
        out_specs=pl.BlockSpec((bm, d), lambda i: (i, 0)),
        out_shape=jax.ShapeDtypeStruct((m, d), out_dtype),
        compiler_params=_cparams(("parallel",)),
        name="rmsnorm",
    )(x, g.reshape(1, d))


def _mm_kernel(x_ref, w_ref, o_ref):
    o_ref[...] = jnp.dot(x_ref[...], w_ref[...], preferred_element_type=_F32).astype(o_ref.dtype)


def _mm(x, w, out_dtype, bm, bn, name):
    m, k = x.shape
    n = w.shape[1]
    return pl.pallas_call(
        _mm_kernel,
        grid=(m // bm, n // bn),
        in_specs=[pl.BlockSpec((bm, k), lambda i, j: (i, 0)), pl.BlockSpec((k, bn), lambda i, j: (0, j))],
        out_specs=pl.BlockSpec((bm, bn), lambda i, j: (i, j)),
        out_shape=jax.ShapeDtypeStruct((m, n), out_dtype),
        compiler_params=_cparams(("parallel", "arbitrary")),
        name=name,
    )(x, w)


def _mm_res_kernel(x_ref, w_ref, r_ref, o_ref):
    o_ref[...] = r_ref[...] + jnp.dot(x_ref[...], w_ref[...], preferred_element_type=_F32)


def _mm_res(x, w, res, bm, bn, name):
    m, k = x.shape
    n = w.shape[1]
    return pl.pallas_call(
        _mm_res_kernel,
        grid=(m // bm, n // bn),
        in_specs=[pl.BlockSpec((bm, k), lambda i, j: (i, 0)), pl.BlockSpec((k, bn), lambda i, j: (0, j)),
                  pl.BlockSpec((bm, bn), lambda i, j: (i, j))],
        out_specs=pl.BlockSpec((bm, bn), lambda i, j: (i, j)),
        out_shape=jax.ShapeDtypeStruct((m, n), _F32),
        compiler_params=_cparams(("parallel", "arbitrary")),
        name=name,
    )(x, w, res)


def _merge_kernel(ya_ref, yr_ref, ym_ref, yg_ref, g0_ref, g1_ref, g2_ref, g3_ref, w_ref, o_ref):
    acc = None
    for b, (y_ref, g_ref) in enumerate(((ya_ref, g0_ref), (yr_ref, g1_ref), (ym_ref, g2_ref), (yg_ref, g3_ref))):
        t = jax.nn.sigmoid(g_ref[...]) * jnp.dot(y_ref[...], w_ref[b], preferred_element_type=_F32)
        acc = t if acc is None else acc + t
    o_ref[...] = acc.astype(o_ref.dtype)


def _merge(ys, proj_main, w_branch, bm, bn):
    m, dm = ys[0].shape
    n = w_branch.shape[2]
    gate_blk0 = (15 * D_MIX) // bn
    per_branch = n // bn

    def gate_spec(b):
        return pl.BlockSpec((bm, bn), lambda i, j: (i, gate_blk0 + b * per_branch + j))

    y_spec = pl.BlockSpec((bm, dm), lambda i, j: (i, 0))
    return pl.pallas_call(
        _merge_kernel,
        grid=(m // bm, n // bn),
        in_specs=[y_spec, y_spec, y_spec, y_spec, gate_spec(0), gate_spec(1), gate_spec(2), gate_spec(3),
                  pl.BlockSpec((N_BRANCH, dm, bn), lambda i, j: (0, 0, j))],
        out_specs=pl.BlockSpec((bm, bn), lambda i, j: (i, j)),
        out_shape=jax.ShapeDtypeStruct((m, n), _BF16),
        compiler_params=_cparams(("parallel", "arbitrary")),
        name="branch_merge",
    )(*ys, proj_main, proj_main, proj_main, proj_main, w_branch)


def _gu_kernel(x_ref, wg_ref, wu_ref, o_ref):
    x = x_ref[...]
    g = jnp.dot(x, wg_ref[...], preferred_element_type=_F32)
    u = jnp.dot(x, wu_ref[...], preferred_element_type=_F32)
    o_ref[...] = (g * jax.nn.sigmoid(g) * u).astype(o_ref.dtype)


def _ffn_gu(x, w_gu, bm, bn):
    m, k = x.shape
    f = w_gu.shape[1] // 2
    nb = f // bn
    return pl.pallas_call(
        _gu_kernel,
        grid=(m // bm, nb),
        in_specs=[pl.BlockSpec((bm, k), lambda i, j: (i, 0)),
                  pl.BlockSpec((k, bn), lambda i, j: (0, j)),
                  pl.BlockSpec((k, bn), lambda i, j: (0, nb + j))],
        out_specs=pl.BlockSpec((bm, bn), lambda i, j: (i, j)),
        out_shape=jax.ShapeDtypeStruct((m, f), _BF16),
        compiler_params=_cparams(("parallel", "arbitrary")),
        name="ffn_gate_up",
    )(x, w_gu, w_gu)


def _down_kernel(x_ref, w_ref, r_ref, o_ref, acc_ref):
    k = pl.program_id(2)

    @pl.when(k == 0)
    def _():
        acc_ref[...] = r_ref[...]

    acc_ref[...] += jnp.dot(x_ref[...], w_ref[...], preferred_element_type=_F32)

    @pl.when(k == pl.num_programs(2) - 1)
    def _():
        o_ref[...] = acc_ref[...]


def _ffn_down(act, w_down, res, bm, bn, bk):
    m, f = act.shape
    n = w_down.shape[1]
    return pl.pallas_call(
        _down_kernel,
        grid=(m // bm, n // bn, f // bk),
        in_specs=[pl.BlockSpec((bm, bk), lambda i, j, k: (i, k)),
                  pl.BlockSpec((bk, bn), lambda i, j, k: (k, j)),
                  pl.BlockSpec((bm, bn), lambda i, j, k: (i, j))],
        out_specs=pl.BlockSpec((bm, bn), lambda i, j, k: (i, j)),
        out_shape=jax.ShapeDtypeStruct((m, n), _F32),
        scratch_shapes=[pltpu.VMEM((bm, bn), _F32)],
        compiler_params=_cparams(("parallel", "arbitrary", "arbitrary")),
        name="ffn_down",
    )(act, w_down, res)


def _router_kernel(x_ref, g_ref, w_ref, h_ref, l_ref):
    x = x_ref[...]
    h = x * lax.rsqrt(jnp.mean(x * x, axis=-1, keepdims=True) + EPS) * g_ref[...]
    h_ref[...] = h.astype(h_ref.dtype)
    logits = jnp.dot(h, w_ref[...], preferred_element_type=_F32, precision=lax.Precision.HIGHEST)
    lane = lax.broadcasted_iota(jnp.int32, logits.shape, 1)
    lg = jnp.where(lane < N_EXPERTS, logits, -jnp.inf)
    v1 = jnp.max(lg, axis=-1, keepdims=True)
    i1 = jnp.min(jnp.where(lg == v1, lane, N_SMALL), axis=-1, keepdims=True)
    lg2 = jnp.where(lane == i1, -jnp.inf, lg)
    v2 = jnp.max(lg2, axis=-1, keepdims=True)
    i2 = jnp.min(jnp.where(lg2 == v2, lane, N_SMALL), axis=-1, keepdims=True)
    e2 = jnp.exp(v2 - v1)
    den = 1.0 + e2
    l_ref[...] = jnp.where(lane == 0, 1.0 / den,
                           jnp.where(lane == 1, e2 / den,
                                     jnp.where(lane == 2, i1.astype(_F32),
                                               jnp.where(lane == 3, i2.astype(_F32), 0.0))))


def _norm_router(x, g, w_router_pad, bm=256):
    m, d = x.shape
    ne = w_router_pad.shape[1]
    return pl.pallas_call(
        _router_kernel,
        grid=(m // bm,),
        in_specs=[pl.BlockSpec((bm, d), lambda i: (i, 0)), pl.BlockSpec((1, d), lambda i: (0, 0)),
                  pl.BlockSpec((d, ne), lambda i: (0, 0))],
        out_specs=[pl.BlockSpec((bm, d), lambda i: (i, 0)), pl.BlockSpec((bm, ne), lambda i: (i, 0))],
        out_shape=[jax.ShapeDtypeStruct((m, d), _BF16), jax.ShapeDtypeStruct((m, ne), _F32)],
        compiler_params=_cparams(("parallel",)),
        name="norm_router",
    )(x, g.reshape(1, d), w_router_pad)


def _moe_gu_kernel(te_ref, nt_ref, x_ref, wg_ref, wu_ref, o_ref):
    @pl.when(pl.program_id(0) < nt_ref[0])
    def _():
        x = x_ref[...]
        g = jnp.dot(x, wg_ref[...], preferred_element_type=_F32)
        u = jnp.dot(x, wu_ref[...], preferred_element_type=_F32)
        o_ref[...] = (g * jax.nn.sigmoid(g) * u).astype(o_ref.dtype)

    @pl.when(pl.program_id(0) >= nt_ref[0])
    def _():
        o_ref[...] = jnp.zeros_like(o_ref)


def _moe_gu(tile_expert, n_tiles, xs, w_gu, bm, bn):
    m, k = xs.shape
    f = w_gu.shape[2] // 2
    nb = f // bn
    grid_spec = pltpu.PrefetchScalarGridSpec(
        num_scalar_prefetch=2,
        grid=(m // bm, nb),
        in_specs=[pl.BlockSpec((bm, k), lambda t, j, te, nt: (t, 0)),
                  pl.BlockSpec((None, k, bn), lambda t, j, te, nt: (te[t], 0, j)),
                  pl.BlockSpec((None, k, bn), lambda t, j, te, nt: (te[t], 0, nb + j))],
        out_specs=pl.BlockSpec((bm, bn), lambda t, j, te, nt: (t, j)),
    )
    return pl.pallas_call(
        _moe_gu_kernel,
        grid_spec=grid_spec,
        out_shape=jax.ShapeDtypeStruct((m, f), _BF16),
        compiler_params=_cparams(("parallel", "arbitrary")),
        name="moe_gate_up",
    )(tile_expert, n_tiles, xs, w_gu, w_gu)


def _moe_down_kernel(te_ref, nt_ref, x_ref, w_ref, o_ref, acc_ref):
    k = pl.program_id(2)

    @pl.when(k == 0)
    def _():
        acc_ref[...] = jnp.zeros_like(acc_ref)

    @pl.when(pl.program_id(0) < nt_ref[0])
    def _():
        acc_ref[...] += jnp.dot(x_ref[...], w_ref[...], preferred_element_type=_F32)

    @pl.when(k == pl.num_programs(2) - 1)
    def _():
        o_ref[...] = acc_ref[...]


def _moe_down(tile_expert, n_tiles, act, w_down, bm, bn, bk):
    m, f = act.shape
    n = w_down.shape[2]
    grid_spec = pltpu.PrefetchScalarGridSpec(
        num_scalar_prefetch=2,
        grid=(m // bm, n // bn, f // bk),
        in_specs=[pl.BlockSpec((bm, bk), lambda t, j, k, te, nt: (t, k)),
                  pl.BlockSpec((None, bk, bn), lambda t, j, k, te, nt: (te[t], k, j))],
        out_specs=pl.BlockSpec((bm, bn), lambda t, j, k, te, nt: (t, j)),
        scratch_shapes=[pltpu.VMEM((bm, bn), _F32)],
    )
    return pl.pallas_call(
        _moe_down_kernel,
        grid_spec=grid_spec,
        out_shape=jax.ShapeDtypeStruct((m, n), _F32),
        compiler_params=_cparams(("parallel", "arbitrary", "arbitrary")),
        name="moe_down",
    )(tile_expert, n_tiles, act, w_down)


def _moe(x, norm_g, w_router, w_gu, w_down):
    m, d = x.shape
    bm = BM_MOE
    wr = jnp.zeros((d, N_SMALL), _F32).at[:, :N_EXPERTS].set(w_router)
    h, route = _norm_router(x, norm_g, wr)
    weights = route[:, 0:TOP_K]
    top_i = route[:, TOP_K:2 * TOP_K].astype(jnp.int32)
    n_pairs = m * TOP_K
    e_flat = top_i.reshape(n_pairs)
    onehot = (e_flat[:, None] == jnp.arange(N_EXPERTS)[None, :]).astype(jnp.int32)
    blk = N_SMALL
    tri = (jnp.arange(blk)[:, None] >= jnp.arange(blk)[None, :]).astype(_F32)
    within = jnp.einsum('ij,bjk->bik', tri, onehot.astype(_F32).reshape(n_pairs // blk, blk, N_EXPERTS),
                        precision=lax.Precision.HIGHEST)
    tot = within[:, -1, :]
    csum = (within + (jnp.cumsum(tot, axis=0) - tot)[:, None, :]).reshape(n_pairs, N_EXPERTS).astype(jnp.int32)
    counts = csum[-1]
    rank = jnp.sum(csum * onehot, axis=1) - 1
    padded = ((counts + bm - 1) // bm) * bm
    ends = jnp.cumsum(padded)
    start = ends - padded
    dest = start[e_flat] + rank
    n_tiles_max = n_pairs // bm + N_EXPERTS
    n_slots = n_tiles_max * bm
    n_tiles = (ends[-1] // bm).astype(jnp.int32)
    src_tok = jnp.zeros((n_slots,), jnp.int32).at[dest].set(jnp.arange(n_pairs, dtype=jnp.int32) // TOP_K)
    tile_start = jnp.arange(n_tiles_max, dtype=jnp.int32) * bm
    tile_expert = jnp.sum((tile_start[:, None] >= ends[None, :]).astype(jnp.int32), axis=1)
    last_e = jnp.max(jnp.where(counts > 0, jnp.arange(N_EXPERTS), 0)).astype(jnp.int32)
    tile_expert = jnp.minimum(tile_expert, last_e).astype(jnp.int32)
    xs = jnp.take(h, src_tok, axis=0)
    act = _moe_gu(tile_expert, n_tiles.reshape(1), xs, w_gu, bm, 1024)
    ys = _moe_down(tile_expert, n_tiles.reshape(1), act, w_down, bm, 1024, 3584)
    dest2 = dest.reshape(m, TOP_K)
    return (x + weights[:, 0:1] * jnp.take(ys, dest2[:, 0], axis=0)
            + weights[:, 1:2] * jnp.take(ys, dest2[:, 1], axis=0))


def _dot(a, b):
    return jnp.dot(a.astype(_BF16), b.astype(_BF16), preferred_element_type=_F32)


def _dot_nt(a, b):
    return lax.dot_general(a.astype(_BF16), b.astype(_BF16), (((1,), (1,)), ((), ())), preferred_element_type=_F32)


def _dot_tn(a, b):
    return lax.dot_general(a.astype(_BF16), b.astype(_BF16), (((0,), (0,)), ((), ())), preferred_element_type=_F32)


def _split2(x):
    hi = x.astype(_BF16)
    return hi, (x - hi.astype(_F32)).astype(_BF16)


def _dot_f32(a, b):
    ah, al = _split2(a)
    bh, bl = _split2(b)
    return (jnp.dot(ah, bh, preferred_element_type=_F32)
            + (jnp.dot(ah, bl, preferred_element_type=_F32) + jnp.dot(al, bh, preferred_element_type=_F32)))


def _dot_cum(tril, x):
    t = tril.astype(_BF16)
    hi = x.astype(_BF16)
    r = x - hi.astype(_F32)
    mid = r.astype(_BF16)
    lo = (r - mid.astype(_F32)).astype(_BF16)
    return (jnp.dot(t, hi, preferred_element_type=_F32)
            + (jnp.dot(t, mid, preferred_element_type=_F32) + jnp.dot(t, lo, preferred_element_type=_F32)))


def _col(x, idx):
    lane = lax.broadcasted_iota(jnp.int32, x.shape, 1)
    return jnp.sum(jnp.where(lane == idx, x, 0.0), axis=-1, keepdims=True)


def _iotas(n):
    return lax.broadcasted_iota(jnp.int32, (n, n), 0), lax.broadcasted_iota(jnp.int32, (n, n), 1)


def _softplus(x):
    return jnp.maximum(x, 0.0) + jnp.log1p(jnp.exp(-jnp.abs(x)))


def _shifted_rows(u, k, carry):
    r = pltpu.roll(u, k, axis=0)
    row = lax.broadcasted_iota(jnp.int32, u.shape, 0)
    nc = carry.shape[0]
    for i in range(k):
        r = jnp.where(row == i, carry[nc - k + i:nc - k + i + 1, :], r)
    return r


def _causal_conv_rows(u, w, carry):
    kw = w.shape[0]
    acc = w[kw - 1:kw, :] * u
    for k in range(1, kw):
        acc = acc + w[kw - 1 - k:kw - k, :] * _shifted_rows(u, k, carry)
    return acc


def _conv_a_kernel(b_ref, c_ref, h_ref, w_ref, hist_ref, y_ref, st_ref, carry_ref):
    @pl.when(pl.program_id(2) == 0)
    def _():
        carry_ref[...] = hist_ref[...]

    u = c_ref[...] * h_ref[...]
    acc = _causal_conv_rows(u, w_ref[...], carry_ref[...])
    y_ref[...] = (b_ref[...] * acc).astype(y_ref.dtype)
    tail = u[u.shape[0] - (CONV_A_WIDTH - 1):, :]
    carry_ref[...] = tail
    st_ref[...] = tail


def _conv_g_kernel(x_ref, w_ref, hist_ref, y_ref, st_ref, carry_ref):
    @pl.when(pl.program_id(2) == 0)
    def _():
        carry_ref[...] = hist_ref[...]

    u = x_ref[...]
    acc = _causal_conv_rows(u, w_ref[...], carry_ref[...])
    y_ref[...] = acc * jax.nn.sigmoid(acc)
    tail = u[u.shape[0] - (GDN_CONV - 1):, :]
    carry_ref[...] = tail
    st_ref[...] = tail


def _update_rows(kernel_fn, y_buf, inputs, in_specs, **kw):
    n_in = len(inputs)

    def body(*refs):
        kernel_fn(*refs[:n_in], *refs[n_in + 1:])

    return pl.pallas_call(body, in_specs=list(in_specs) + [pl.BlockSpec(memory_space=pl.ANY)],
                          input_output_aliases={n_in: 0}, **kw)(*inputs, y_buf)


def _conv_calls(proj_main, row0, nb, t, tb, conv_a_w, hist_a, gdn_conv_w, hist_g, ya_buf):
    nt = t // tb
    r0 = row0 // tb
    cb = D_MIX

    def rows(c0):
        return pl.BlockSpec((tb, cb), lambda b, j, i: (r0 + b * nt + i, c0 + j))

    ka, kg = CONV_A_WIDTH, GDN_CONV
    y_a, st_a = _update_rows(
        _conv_a_kernel, ya_buf, (proj_main, proj_main, proj_main, conv_a_w, hist_a),
        [rows(0), rows(1), rows(2),
         pl.BlockSpec((ka, cb), lambda b, j, i: (0, j)),
         pl.BlockSpec((None, ka - 1, cb), lambda b, j, i: (b, 0, j))],
        grid=(nb, 1, nt),
        out_specs=[pl.BlockSpec((tb, cb), lambda b, j, i: (r0 + b * nt + i, j)),
                   pl.BlockSpec((None, ka - 1, cb), lambda b, j, i: (b, 0, j))],
        out_shape=[jax.ShapeDtypeStruct(ya_buf.shape, _BF16),
                   jax.ShapeDtypeStruct((nb, ka - 1, D_MIX), _F32)],
        scratch_shapes=[pltpu.VMEM((ka - 1, cb), _F32)],
        compiler_params=_cparams(("parallel", "parallel", "arbitrary")),
        name="conv_a",
    )
    gqkv, st_g = pl.pallas_call(
        _conv_g_kernel,
        grid=(nb, 3, nt),
        in_specs=[rows(11),
                  pl.BlockSpec((kg, cb), lambda b, j, i: (0, j)),
                  pl.BlockSpec((None, kg - 1, cb), lambda b, j, i: (b, 0, j))],
        out_specs=[pl.BlockSpec((tb, cb), lambda b, j, i: (b * nt + i, j)),
                   pl.BlockSpec((None, kg - 1, cb), lambda b, j, i: (b, 0, j))],
        out_shape=[jax.ShapeDtypeStruct((nb * t, 3 * D_MIX), _F32),
                   jax.ShapeDtypeStruct((nb, kg - 1, 3 * D_MIX), _F32)],
        scratch_shapes=[pltpu.VMEM((kg - 1, cb), _F32)],
        compiler_params=_cparams(("parallel", "parallel", "arbitrary")),
        name="conv_gdn",
    )(proj_main, gdn_conv_w, hist_g)
    return y_a, st_a, gqkv, st_g


def _head_norm_gate(o, gain, gate):
    oc = o - jnp.mean(o, axis=-1, keepdims=True)
    y = oc * lax.rsqrt(jnp.mean(oc * oc, axis=-1, keepdims=True) + EPS)
    return gate * (y * gain)


def _ret_kernel(q_ref, k_ref, v_ref, g_ref, cos_ref, sin_ref, ng_ref, lg_ref, s0_ref, y_ref, s_ref):
    L = q_ref.shape[0]

    @pl.when(pl.program_id(2) == 0)
    def _():
        s_ref[...] = s0_ref[...]

    lg = lg_ref[...][:, :1]
    cos, sin = cos_ref[...], sin_ref[...]
    half = RET_DK // 2

    def rope(x):
        x1, x2 = x[:, :half], x[:, half:]
        return jnp.concatenate([x1 * cos - x2 * sin, x1 * sin + x2 * cos], axis=-1)

    rq = rope(q_ref[...])
    rk = rope(k_ref[...]) * RET_DK ** -0.5
    v = v_ref[...]
    row, col = _iotas(L)
    intra = jnp.where(row >= col, jnp.exp((row - col).astype(_F32) * lg), 0.0)
    ridx = lax.broadcasted_iota(jnp.int32, (L, 1), 0).astype(_F32)
    inter = jnp.exp((ridx + 1.0) * lg)
    w_k = jnp.exp((L - 1.0 - ridx) * lg)
    dec = jnp.exp(L * lg)
    S = s_ref[...]
    s = _dot_nt(rq, rk) * intra
    o = _dot(s, v) + inter * _dot(rq, S)
    s_ref[...] = dec * S + _dot_tn(rk * w_k, v)
    g = g_ref[...]
    y_ref[...] = _head_norm_gate(o, ng_ref[...], g * jax.nn.sigmoid(g)).astype(y_ref.dtype)


def _mlstm_kernel(q_ref, k_ref, v_ref, og_ref, ps_ref, bias_ref, ng_ref, c0_ref, n0_ref, m0_ref,
                  y_ref, c_ref, n_ref, m_ref):
    L = q_ref.shape[0]
    h = pl.program_id(1)

    @pl.when(pl.program_id(2) == 0)
    def _():
        c_ref[...] = c0_ref[...]
        n_ref[...] = n0_ref[...]
        m_ref[...] = m0_ref[...]

    z = ps_ref[...] + bias_ref[...]
    li = _col(z, _PS_MI + h)
    fp = _col(z, _PS_MF + h)
    lf = jnp.minimum(fp, 0.0) - jnp.log1p(jnp.exp(-jnp.abs(fp)))
    row, col = _iotas(L)
    causal = row >= col
    tril = causal.astype(_F32)
    logd = _dot_cum(tril, jnp.where(row > col, lf, jnp.where(row == col, li, 0.0)))
    b = (logd[:, :1] - li[0:1, :]) + lf[0:1, :]
    m_prev = m_ref[...][:, :1]
    inter = b + m_prev
    mt = jnp.maximum(inter, jnp.max(jnp.where(causal, logd, -jnp.inf), axis=-1, keepdims=True))
    dmat = jnp.where(causal, jnp.exp(logd - mt), 0.0)
    q = q_ref[...]
    k = k_ref[...] * MLSTM_DK ** -0.5
    v = v_ref[...]
    s = _dot_nt(q, k) * dmat
    g_inter = jnp.exp(inter - mt)
    C = c_ref[...]
    n = n_ref[...]
    num = _dot(s, v) + g_inter * _dot(q, C)
    den = jnp.sum(s, axis=-1, keepdims=True) + g_inter * jnp.sum(q * n, axis=-1, keepdims=True)
    hh = num / jnp.maximum(jnp.abs(den), 1.0)
    m_last = mt[L - 1:L, :]
    b_last = b[L - 1:L, :]
    kw = k * jnp.exp(b_last - b + li - m_last)
    d_c = jnp.exp(b_last + m_prev - m_last)
    c_ref[...] = d_c * C + _dot_tn(kw, v)
    n_ref[...] = d_c * n + jnp.sum(kw, axis=0, keepdims=True)
    m_ref[...] = jnp.broadcast_to(m_last, m_ref.shape)
    y_ref[...] = _head_norm_gate(hh, ng_ref[...], jax.nn.sigmoid(og_ref[...])).astype(y_ref.dtype)


def _unit_lower_inverse(a, n):
    row, col = _iotas(n)
    eye = (row == col).astype(_F32)

    def same_block(shift):
        return jnp.right_shift(row, shift) == jnp.right_shift(col, shift)

    ad = jnp.where(same_block(3), a, 0.0)
    a2 = _dot_f32(ad, ad)
    a4 = _dot_f32(a2, a2)
    t = _dot_f32(_dot_f32(eye - ad, eye + a2), eye + a4)
    shift = 3
    while (1 << shift) < n:
        off = jnp.logical_and(same_block(shift + 1), jnp.logical_not(same_block(shift)))
        t = t - _dot_f32(_dot_f32(t, jnp.where(off, a, 0.0)), t)
        shift += 1
    return t


def _gdn_kernel(q_ref, k_ref, v_ref, z_ref, ps_ref, alog_ref, dtb_ref, ng_ref, s0_ref, y_ref, s_ref):
    L = q_ref.shape[0]

    @pl.when(pl.program_id(2) == 0)
    def _():
        s_ref[...] = s0_ref[...]

    ps = ps_ref[...]
    ps_a = ps + dtb_ref[...]
    row, col = _iotas(L)
    incl = row >= col
    strict = row > col
    tril = incl.astype(_F32)

    def l2n(x):
        return x * lax.rsqrt(jnp.sum(x * x, axis=-1, keepdims=True) + EPS)

    for j in range(GDN_HEADS_PER_STEP):
        h = pl.program_id(1) * GDN_HEADS_PER_STEP + j
        lanes = slice(j * GDN_DK, (j + 1) * GDN_DK)
        beta = jax.nn.sigmoid(_col(ps, _PS_BETA + h))
        neg_a = -jnp.exp(_col(alog_ref[...], _PS_A + h))
        g = neg_a * _softplus(_col(ps_a, _PS_A + h))
        q = l2n(q_ref[:, lanes]) * GDN_DK ** -0.5
        k = l2n(k_ref[:, lanes])
        v = v_ref[:, lanes]
        gd = _dot_cum(tril, jnp.where(strict, g, 0.0))
        G = gd[:, :1] + g[0:1, :]
        decay = jnp.where(incl, jnp.exp(gd), 0.0)
        a = jnp.where(strict, beta * _dot_nt(k, k) * decay, 0.0)
        e_g = jnp.exp(G)
        rhs = jnp.concatenate([v * beta, k * (beta * e_g)], axis=-1)
        sol = _dot_f32(_unit_lower_inverse(a, L), rhs)
        u, w = sol[:, :GDN_DK], sol[:, GDN_DK:]
        S = s_ref[j]
        delta = u - _dot(w, S)
        o = _dot(_dot_nt(q, k) * decay, delta) + e_g * _dot(q, S)
        g_last = G[L - 1:L, :]
        s_ref[j] = jnp.exp(g_last) * S + _dot_tn(k * jnp.exp(g_last - G), delta)
        go = o * lax.rsqrt(jnp.mean(o * o, axis=-1, keepdims=True) + EPS) * ng_ref[...]
        z = z_ref[:, lanes]
        y_ref[:, lanes] = (go * (z * jax.nn.sigmoid(z))).astype(y_ref.dtype)


def _rope_tables(pos):
    half = RET_DK // 2
    inv = ROPE_BASE ** (-jnp.arange(half, dtype=_F32) / half)
    ang = pos.astype(_F32)[:, None] * inv[None, :]
    return jnp.cos(ang), jnp.sin(ang)


def _lane_row(vals, off):
    return jnp.zeros((1, N_SMALL), _F32).at[0, off:off + vals.shape[0]].set(vals)


def _mixers(proj_main, proj_small, y_bufs, row0, nb, t, L, tb, pos, st, conv_a_w, ret_norm_g, mlstm_b_i, mlstm_b_f,
            mlstm_norm_g, gdn_conv_w, gdn_a_log, gdn_dt_bias, gdn_norm_g):
    conv_a_buf, ret_S, m_C, m_n, m_m, g_buf, g_S = st
    nc = t // L
    r0 = row0 // L
    sem = ("parallel", "parallel", "arbitrary")
    y_shape = jax.ShapeDtypeStruct(y_bufs[0].shape, _BF16)
    y_a, st_a, gqkv, st_g = _conv_calls(proj_main, row0, nb, t, tb, conv_a_w, conv_a_buf, gdn_conv_w, g_buf,
                                        y_bufs[0])

    def main_cols(group, width, heads):
        return pl.BlockSpec((L, width), lambda b, h, c: (r0 + b * nc + c, group * heads + h))

    def out_cols(width):
        return pl.BlockSpec((L, width), lambda b, h, c: (r0 + b * nc + c, h))

    def state_spec(*dims):
        return pl.BlockSpec((None, None) + dims, lambda b, h, c: (b, h) + (0,) * len(dims))

    def gain_spec(width):
        return pl.BlockSpec((1, width), lambda b, h, c: (0, h))

    small_spec = pl.BlockSpec((L, N_SMALL), lambda b, h, c: (r0 + b * nc + c, 0))
    lane_spec = pl.BlockSpec((1, N_SMALL), lambda b, h, c: (0, 0))

    cos, sin = _rope_tables(pos)
    log_gamma = jnp.log1p(-jnp.exp2(-5.0 - jnp.arange(RET_HEADS, dtype=_F32)))
    lg_tab = jnp.broadcast_to(log_gamma[:, None, None], (RET_HEADS, 1, N_SMALL))
    rope_spec = pl.BlockSpec((L, RET_DK // 2), lambda b, h, c: (c, 0))
    y_r, ret_new = _update_rows(
        _ret_kernel, y_bufs[1],
        (proj_main, proj_main, proj_main, proj_main, cos, sin, ret_norm_g.reshape(1, D_MIX), lg_tab, ret_S),
        [main_cols(3, RET_DK, RET_HEADS), main_cols(4, RET_DK, RET_HEADS),
         main_cols(5, RET_DK, RET_HEADS), main_cols(6, RET_DK, RET_HEADS),
         rope_spec, rope_spec, gain_spec(RET_DK),
         pl.BlockSpec((None, 1, N_SMALL), lambda b, h, c: (h, 0, 0)),
         state_spec(RET_DK, RET_DK)],
        grid=(nb, RET_HEADS, nc),
        out_specs=[out_cols(RET_DK), state_spec(RET_DK, RET_DK)],
        out_shape=[y_shape, jax.ShapeDtypeStruct((nb, RET_HEADS, RET_DK, RET_DK), _F32)],
        compiler_params=_cparams(sem),
        name="retention",
    )

    bias_row = _lane_row(jnp.concatenate([mlstm_b_i, mlstm_b_f]), _PS_MI)
    y_m, c_new, n_new, m_new = _update_rows(
        _mlstm_kernel, y_bufs[2],
        (proj_main, proj_main, proj_main, proj_main, proj_small, bias_row, mlstm_norm_g.reshape(1, D_MIX),
         m_C, m_n.reshape(nb, MLSTM_HEADS, 1, MLSTM_DK),
         jnp.broadcast_to(m_m[:, :, None, None], (nb, MLSTM_HEADS, 1, N_SMALL))),
        [main_cols(7, MLSTM_DK, MLSTM_HEADS), main_cols(8, MLSTM_DK, MLSTM_HEADS),
         main_cols(9, MLSTM_DK, MLSTM_HEADS), main_cols(10, MLSTM_DK, MLSTM_HEADS),
         small_spec, lane_spec, gain_spec(MLSTM_DK),
         state_spec(MLSTM_DK, MLSTM_DK), state_spec(1, MLSTM_DK), state_spec(1, N_SMALL)],
        grid=(nb, MLSTM_HEADS, nc),
        out_specs=[out_cols(MLSTM_DK), state_spec(MLSTM_DK, MLSTM_DK), state_spec(1, MLSTM_DK),
                   state_spec(1, N_SMALL)],
        out_shape=[y_shape,
                   jax.ShapeDtypeStruct((nb, MLSTM_HEADS, MLSTM_DK, MLSTM_DK), _F32),
                   jax.ShapeDtypeStruct((nb, MLSTM_HEADS, 1, MLSTM_DK), _F32),
                   jax.ShapeDtypeStruct((nb, MLSTM_HEADS, 1, N_SMALL), _F32)],
        compiler_params=_cparams(sem),
        name="mlstm",
    )

    hps = GDN_HEADS_PER_STEP
    n_hg = GDN_HEADS // hps

    def conv_cols(group):
        return pl.BlockSpec((L, hps * GDN_DK), lambda b, h, c: (b * nc + c, group * n_hg + h))

    gdn_state = pl.BlockSpec((None, hps, GDN_DK, GDN_DK), lambda b, h, c: (b, h, 0, 0))
    y_g, gdn_new = _update_rows(
        _gdn_kernel, y_bufs[3],
        (gqkv, gqkv, gqkv, proj_main, proj_small, _lane_row(gdn_a_log, _PS_A), _lane_row(gdn_dt_bias, _PS_A),
         gdn_norm_g.reshape(1, GDN_DK), g_S),
        [conv_cols(0), conv_cols(1), conv_cols(2), main_cols(14, hps * GDN_DK, n_hg),
         small_spec, lane_spec, lane_spec,
         pl.BlockSpec((1, GDN_DK), lambda b, h, c: (0, 0)),
         gdn_state],
        grid=(nb, n_hg, nc),
        out_specs=[out_cols(hps * GDN_DK), gdn_state],
        out_shape=[y_shape, jax.ShapeDtypeStruct((nb, GDN_HEADS, GDN_DK, GDN_DK), _F32)],
        compiler_params=_cparams(sem),
        name="gated_delta",
    )

    new_state = (st_a, ret_new, c_new, n_new.reshape(nb, MLSTM_HEADS, MLSTM_DK), m_new[:, :, 0, 0], st_g, gdn_new)
    return (y_a, y_r, y_m, y_g), new_state


def kernel(x_prompt, x_sample, state_conv_a, state_ret, state_mlstm_c, state_mlstm_n, state_mlstm_m, state_gdn_conv, state_gdn, norm1_g, norm2_g, norm_f_g, w_in, conv_a_w, ret_norm_g, mlstm_b_i, mlstm_b_f, mlstm_norm_g, gdn_conv_w, gdn_a_log, gdn_dt_bias, gdn_norm_g, w_branch, w_out, ffn_w_gu, ffn_w_down, moe_w_router, moe_w_gu, moe_w_down):
    bp, tp, d = x_prompt.shape
    bs, ts, _ = x_sample.shape
    mp, ms = bp * tp, bs * ts
    m = mp + ms
    x = jnp.concatenate([x_prompt.reshape(mp, d), x_sample.reshape(ms, d)], axis=0)
    pos_p = jnp.arange(tp)
    pos_s = PAST_LEN + jnp.arange(ts)
    sample_state = (state_conv_a, state_ret, state_mlstm_c, state_mlstm_n, state_mlstm_m, state_gdn_conv, state_gdn)
    chunk_p = MIX_CHUNK if tp % MIX_CHUNK == 0 else tp
    conv_rows_p = CONV_ROWS if tp % CONV_ROWS == 0 else tp
    new_p, new_s = [], []
    for l in range(DEPTH):
        wl = w_in[l]
        w_main = jnp.concatenate([wl[:, :_OFF_MI], wl[:, _OFF_GQKV:_OFF_GBETA], wl[:, _OFF_GATE:]], axis=1).astype(_BF16)
        w_small = jnp.concatenate([wl[:, _OFF_MI:_OFF_GQKV], wl[:, _OFF_GBETA:_OFF_GATE],
                                   jnp.zeros((d, N_SMALL - 2 * MLSTM_HEADS - 2 * GDN_HEADS), _F32)], axis=1).astype(_BF16)
        h = _rmsnorm(x, norm1_g[l], _BF16)
        proj_main = _mm(h, w_main, _F32, BM, 1024, "in_proj")
        proj_small = _mm(h, w_small, _F32, BM, N_SMALL, "in_proj_gates")
        mixer_w = (conv_a_w[l], ret_norm_g[l], mlstm_b_i[l], mlstm_b_f[l], mlstm_norm_g[l],
                   gdn_conv_w[l], gdn_a_log[l], gdn_dt_bias[l], gdn_norm_g[l])
        init_p = tuple(jnp.zeros((bp,) + s.shape[2:], _F32) for s in sample_state)
        ys = tuple(jnp.zeros((m, D_MIX), _BF16) for _ in range(N_BRANCH))
        ys, st_p = _mixers(proj_main, proj_small, ys, 0, bp, tp, chunk_p, conv_rows_p, pos_p, init_p, *mixer_w)
        ys, st_s = _mixers(proj_main, proj_small, ys, mp, bs, ts, ts, ts, pos_s,
                           tuple(s[l] for s in sample_state), *mixer_w)
        new_p.append(st_p)
        new_s.append(st_s)
        merged = _merge(ys, proj_main, w_branch[l].astype(_BF16), BM, 512)
        x = _mm_res(merged, w_out[l].astype(_BF16), x, BM, 1024, "out_proj")
        if l % 2 == 0:
            h2 = _rmsnorm(x, norm2_g[l], _BF16)
            act = _ffn_gu(h2, ffn_w_gu[l // 2].astype(_BF16), BM, 512)
            x = _ffn_down(act, ffn_w_down[l // 2].astype(_BF16), x, BM, 1024, 3584)
        else:
            x = _moe(x, norm2_g[l], moe_w_router[l // 2], moe_w_gu[l // 2].astype(_BF16),
                     moe_w_down[l // 2].astype(_BF16))
    y = _rmsnorm(x, norm_f_g, _F32)
    outs = [y[:mp].reshape(bp, tp, d), y[mp:].reshape(bs, ts, d)]
    for i in range(7):
        outs.append(jnp.stack([new_p[l][i] for l in range(DEPTH)]))
        outs.append(jnp.stack([new_s[l][i] for l in range(DEPTH)]))
    return tuple(outs)
```
